```python
import jax
import jax.numpy as jnp
from jax import lax
import numpy as np

D_MODEL = 1024
BATCH = 2
SEQ = 8192
DEPTH = 4
DEC_BATCH = 32
DEC_SEQ = 1
PAST_LEN = 8192
PAGE_SIZE = 128

N_HEADS = 8
HEAD_DIM = 64
D_ATTN = N_HEADS * HEAD_DIM
D_POOL = D_MODEL - D_ATTN
POOL_WINDOWS = (2, 4, 8, 16)
N_POOL_GROUPS = len(POOL_WINDOWS)
POOL_GROUP_DIM = D_POOL // N_POOL_GROUPS
POOL_BUF = max(POOL_WINDOWS) - 1
DILATION_PATTERNS = ((128, 1), (512, 4), (2048, 16))
MAX_WINDOW = max(w for w, _ in DILATION_PATTERNS)
BLOCK = 128
D_IN = 3 * D_ATTN + D_POOL
D_FF = 4 * D_MODEL
D_PLE = 256
EPS = 1e-6
NEG_INF = -1e30
ATTN_SCALE = HEAD_DIM ** -0.5

kernel_name = 'hymba_dilated_pool_decoder_step'


def rmsnorm(x, g):
    xf = x.astype(jnp.float32)
    y = xf * lax.rsqrt(jnp.mean(xf * xf, axis=-1, keepdims=True) + EPS)
    return (y * g.astype(jnp.float32)).astype(x.dtype)


def _project(xn, w_in):
    B, T, _ = xn.shape
    proj = xn @ w_in
    q = proj[..., :D_ATTN].reshape(B, T, N_HEADS, HEAD_DIM)
    k = proj[..., D_ATTN:2 * D_ATTN].reshape(B, T, N_HEADS, HEAD_DIM)
    v = proj[..., 2 * D_ATTN:3 * D_ATTN].reshape(B, T, N_HEADS, HEAD_DIM)
    u = proj[..., 3 * D_ATTN:]
    return q, k, v, u


def _dilated_attn_prompt(q, k, v, window, dil):
    B, S, H, E = q.shape
    span = window // dil
    lc = -(-S // (dil * BLOCK)) * BLOCK
    nb = lc // BLOCK
    pad = lc * dil - S

    def to_classes(t):
        t = jnp.pad(t, ((0, 0), (0, pad), (0, 0), (0, 0)))
        t = t.reshape(B, lc, dil, H, E).transpose(0, 2, 3, 1, 4)
        return t.reshape(B, dil, H, nb, BLOCK, E)

    def with_prev(t):
        prev = jnp.pad(t, ((0, 0), (0, 0), (0, 0), (1, 0), (0, 0), (0, 0)))[:, :, :, :nb]
        return jnp.concatenate([prev, t], axis=4)

    qb = to_classes(q)
    kb = with_prev(to_classes(k))
    vb = with_prev(to_classes(v))
    s = jnp.einsum('bdhnqe,bdhnke->bdhnqk', qb, kb).astype(jnp.float32) * ATTN_SCALE
    qi = jnp.arange(BLOCK)[:, None]
    kj = jnp.arange(2 * BLOCK)[None, :] - BLOCK
    rel = qi - kj
    key_idx = jnp.arange(nb)[:, None, None] * BLOCK + kj[None]
    mask = (rel >= 0) & (rel <= span) & (key_idx >= 0)
    s = jnp.where(mask, s, NEG_INF)
    m = jnp.max(s, axis=-1, keepdims=True)
    p = jnp.exp(s - m)
    den = jnp.sum(p, axis=-1)
    o = jnp.einsum('bdhnqk,bdhnke->bdhnqe', p, vb.astype(jnp.float32)) / den[..., None]
    lse = m[..., 0] + jnp.log(den)
    o = o.reshape(B, dil, H, lc, E).transpose(0, 3, 1, 2, 4).reshape(B, lc * dil, H, E)[:, :S]
    lse = lse.reshape(B, dil, H, lc).transpose(0, 3, 1, 2).reshape(B, lc * dil, H)[:, :S]
    return o, lse


def _dilated_attn_sample(q, k_ext, v_ext, window, dil):
    T = q.shape[1]
    n_hist = k_ext.shape[1] - T
    span = window // dil
    idx = (n_hist + jnp.arange(T))[:, None] - dil * jnp.arange(span + 1)[None, :]
    valid = idx >= 0
    idx = jnp.maximum(idx, 0)
    kg = k_ext[:, idx]
    vg = v_ext[:, idx]
    s = jnp.einsum('bthe,btkhe->bthk', q, kg).astype(jnp.float32) * ATTN_SCALE
    s = jnp.where(valid[None, :, None, :], s, NEG_INF)
    m = jnp.max(s, axis=-1, keepdims=True)
    p = jnp.exp(s - m)
    den = jnp.sum(p, axis=-1)
    o = jnp.einsum('bthk,btkhe->bthe', p, vg.astype(jnp.float32)) / den[..., None]
    lse = m[..., 0] + jnp.log(den)
    return o, lse


def _pool_mix(u_ext, n_new, pool_w, pool_scale):
    B, R, _ = u_ext.shape
    uf = u_ext.astype(jnp.float32)
    cs = jnp.pad(jnp.cumsum(uf, axis=1), ((0, 0), (1, 0), (0, 0)))
    rows = jnp.arange(R - n_new, R)
    cur = uf[:, R - n_new:]
    groups = []
    for g, w in enumerate(POOL_WINDOWS):
        sl = slice(g * POOL_GROUP_DIM, (g + 1) * POOL_GROUP_DIM)
        lo = jnp.maximum(rows + 1 - w, 0)
        cnt = (rows + 1 - lo).astype(jnp.float32)
        mean = (cs[:, rows + 1, sl] - cs[:, lo, sl]) / cnt[None, :, None]
        groups.append(mean - cur[..., sl])
    z = jnp.stack(groups, axis=2)
    y = jnp.einsum('bngc,gcd->bngd', z, pool_w.astype(jnp.float32))
    y = y.reshape(B, n_new, D_POOL) * pool_scale.astype(jnp.float32)
    return y.astype(u_ext.dtype)


def _mixer_out(outs, lses, pooled, w_out):
    wts = jax.nn.softmax(jnp.stack(lses, axis=0), axis=0)
    attn = jnp.einsum('gbth,gbthe->bthe', wts, jnp.stack(outs, axis=0))
    B, T = attn.shape[:2]
    attn = attn.reshape(B, T, D_ATTN).astype(pooled.dtype)
    return jnp.concatenate([attn, pooled], axis=-1) @ w_out


def _mlp(h, g, w_up, w_down):
    a = jax.nn.relu(rmsnorm(h, g) @ w_up)
    return (a * a) @ w_down


def _ple(h, p, g, w_gate, w_ple):
    gate = jax.nn.sigmoid(rmsnorm(h, g) @ w_gate)
    return (p @ w_ple) * gate


def setup_inputs(seed: int = 0) -> dict:
    key = jax.random.key(seed)
    ks = jax.random.split(key, 20)
    nrm = jax.random.normal
    w_buf = min(MAX_WINDOW, PAST_LEN)
    return {
        'x_prompt': nrm(ks[0], (BATCH, SEQ, D_MODEL), jnp.float32),
        'x_sample': nrm(ks[1], (DEC_BATCH, DEC_SEQ, D_MODEL), jnp.float32),
        'cache_attn_kv': nrm(ks[2], (DEPTH, DEC_BATCH, w_buf, 2, N_HEADS, HEAD_DIM), jnp.float32),
        'state_pool': nrm(ks[3], (DEPTH, DEC_BATCH, POOL_BUF, D_POOL), jnp.float32),
        'p_prompt': nrm(ks[4], (DEPTH, BATCH, SEQ, D_PLE), jnp.float32),
        'p_sample': nrm(ks[5], (DEPTH, DEC_BATCH, DEC_SEQ, D_PLE), jnp.float32),
        'norm_attn_g': 1.0 + 0.05 * nrm(ks[6], (DEPTH, D_MODEL), jnp.float32),
        'w_in': nrm(ks[7], (DEPTH, D_MODEL, D_IN), jnp.float32) * D_MODEL ** -0.5,
        'pool_w': nrm(ks[8], (DEPTH, N_POOL_GROUPS, POOL_GROUP_DIM, POOL_GROUP_DIM), jnp.float32) * POOL_GROUP_DIM ** -0.5,
        'pool_scale': 1.0 + 0.1 * nrm(ks[9], (DEPTH, D_POOL), jnp.float32),
        'w_out': nrm(ks[10], (DEPTH, D_MODEL, D_MODEL), jnp.float32) * D_MODEL ** -0.5,
        'norm_mlp_g': 1.0 + 0.05 * nrm(ks[11], (DEPTH, D_MODEL), jnp.float32),
        'w_up': nrm(ks[12], (DEPTH, D_MODEL, D_FF), jnp.float32) * D_MODEL ** -0.5,
        'w_down': nrm(ks[13], (DEPTH, D_FF, D_MODEL), jnp.float32) * D_FF ** -0.5,
        'ple_norm_g': 1.0 + 0.05 * nrm(ks[14], (DEPTH, D_MODEL), jnp.float32),
        'w_ple_gate': nrm(ks[15], (DEPTH, D_MODEL, D_MODEL), jnp.float32) * D_MODEL ** -0.5,
        'w_ple': nrm(ks[16], (DEPTH, D_PLE, D_MODEL), jnp.float32) * D_PLE ** -0.5,
        'final_norm_g': 1.0 + 0.05 * nrm(ks[17], (D_MODEL,), jnp.float32),
    }


def reference(x_prompt, x_sample, cache_attn_kv, state_pool, p_prompt, p_sample,
              norm_attn_g, w_in, pool_w, pool_scale, w_out, norm_mlp_g, w_up, w_down,
              ple_norm_g, w_ple_gate, w_ple, final_norm_g):
    hp = x_prompt
    hs = x_sample
    S = x_prompt.shape[1]
    T = x_sample.shape[1]
    kv_keep = min(MAX_WINDOW, S)
    kv_p_list, kv_s_list, pool_p_list, pool_s_list = [], [], [], []
    for i in range(DEPTH):
        q, k, v, u = _project(rmsnorm(hp, norm_attn_g[i]), w_in[i])
        outs, lses = [], []
        for window, dil in DILATION_PATTERNS:
            o, l = _dilated_attn_prompt(q, k, v, window, dil)
            outs.append(o)
            lses.append(l)
        pooled = _pool_mix(u, S, pool_w[i], pool_scale[i])
        hp = hp + _mixer_out(outs, lses, pooled, w_out[i])
        hp = hp + _mlp(hp, norm_mlp_g[i], w_up[i], w_down[i])
        hp = hp + _ple(hp, p_prompt[i], ple_norm_g[i], w_ple_gate[i], w_ple[i])
        kv_p_list.append(jnp.stack([k, v], axis=2)[:, S - kv_keep:])
        pool_p_list.append(u[:, S - POOL_BUF:])

        q, k, v, u = _project(rmsnorm(hs, norm_attn_g[i]), w_in[i])
        kv_c = cache_attn_kv[i].astype(k.dtype)
        k_ext = jnp.concatenate([kv_c[:, :, 0], k], axis=1)
        v_ext = jnp.concatenate([kv_c[:, :, 1], v], axis=1)
        outs, lses = [], []
        for window, dil in DILATION_PATTERNS:
            o, l = _dilated_attn_sample(q, k_ext, v_ext, window, dil)
            outs.append(o)
            lses.append(l)
        u_ext = jnp.concatenate([state_pool[i].astype(u.dtype), u], axis=1)
        pooled = _pool_mix(u_ext, T, pool_w[i], pool_scale[i])
        hs = hs + _mixer_out(outs, lses, pooled, w_out[i])
        hs = hs + _mlp(hs, norm_mlp_g[i], w_up[i], w_down[i])
        hs = hs + _ple(hs, p_sample[i], ple_norm_g[i], w_ple_gate[i], w_ple[i])
        kv_s_list.append(jnp.stack([k, v], axis=2))
        pool_s_list.append(u_ext[:, -POOL_BUF:])

    y_prompt = rmsnorm(hp, final_norm_g)
    y_sample = rmsnorm(hs, final_norm_g)
    kv_prompt = jnp.stack(kv_p_list, axis=0)
    kv_sample = jnp.stack(kv_s_list, axis=0)
    pool_prompt = jnp.stack(pool_p_list, axis=0)
    pool_sample = jnp.stack(pool_s_list, axis=0)
    return (y_prompt, y_sample, kv_prompt, kv_sample, pool_prompt, pool_sample)
```

```python
import functools

import jax
import jax.numpy as jnp
from jax import lax
from jax.experimental import pallas as pl
from jax.experimental.pallas import tpu as pltpu

N_HEADS = 8
HEAD_DIM = 64
D_ATTN = N_HEADS * HEAD_DIM
POOL_WINDOWS = (2, 4, 8, 16)
POOL_HALO = 16
DILATIONS = (1, 4, 16)
SPAN = 128
QBLK = 128
LANES = 128
HEAD_ROWS = 16
EPS = 1e-6
NEG_INF = -1e30
ATTN_SCALE = HEAD_DIM ** -0.5
VMEM_LIMIT = 56 * 1024 * 1024


def _rms(x, g):
    return x * lax.rsqrt(jnp.mean(x * x, axis=-1, keepdims=True) + EPS) * g


def _dot(a, b):
    return jnp.dot(a, b, preferred_element_type=jnp.float32)


def _dot_nt(a, b):
    return lax.dot_general(a, b, (((1,), (1,)), ((), ())), preferred_element_type=jnp.float32)


def _const_spec(shape):
    nd = len(shape)
    return pl.BlockSpec(shape, lambda *_: (0,) * nd, pipeline_mode=pl.Buffered(1))


def _params(n_grid):
    return pltpu.CompilerParams(dimension_semantics=("arbitrary",) * n_grid,
                                vmem_limit_bytes=VMEM_LIMIT)


def _inproj_body(x_ref, g_ref, w_ref, q_ref, kv_ref, kvf_ref, u_ref):
    xn = _rms(x_ref[...], g_ref[...]).astype(jnp.bfloat16)
    proj = _dot(xn, w_ref[...])
    q_ref[...] = (proj[:, :D_ATTN] * ATTN_SCALE).astype(jnp.bfloat16)
    kvf = proj[:, D_ATTN:3 * D_ATTN]
    kvf_ref[...] = kvf
    kv_ref[...] = kvf.astype(jnp.bfloat16)
    u_ref[...] = proj[:, 3 * D_ATTN:]


def _inproj(x, g, w, tm):
    n, d = x.shape
    d_in = w.shape[1]
    d_pool = d_in - 3 * D_ATTN
    row = lambda c: pl.BlockSpec((tm, c), lambda i: (i, 0))
    return pl.pallas_call(
        _inproj_body,
        grid=(n // tm,),
        in_specs=[row(d), _const_spec((1, d)), _const_spec((d, d_in))],
        out_specs=[row(D_ATTN), row(2 * D_ATTN), row(2 * D_ATTN), row(d_pool)],
        out_shape=[jax.ShapeDtypeStruct((n, D_ATTN), jnp.bfloat16),
                   jax.ShapeDtypeStruct((n, 2 * D_ATTN), jnp.bfloat16),
                   jax.ShapeDtypeStruct((n, 2 * D_ATTN), jnp.float32),
                   jax.ShapeDtypeStruct((n, d_pool), jnp.float32)],
        compiler_params=_params(1),
        name="inproj",
    )(x, g, w)


def _attn_body(q_ref, kc_ref, kp_ref, vc_ref, vp_ref, o_ref, l_ref):
    n = pl.program_id(2)
    qi = lax.broadcasted_iota(jnp.int32, (QBLK, 2 * QBLK), 0)
    kj = lax.broadcasted_iota(jnp.int32, (QBLK, 2 * QBLK), 1) - QBLK
    rel = qi - kj
    mask = (rel >= 0) & (rel <= SPAN) & ((kj >= 0) | (n > 0))
    lane = lax.broadcasted_iota(jnp.int32, (QBLK, LANES), 1)
    low = lane < HEAD_DIM
    for p in range(D_ATTN // LANES):
        cols = slice(p * LANES, (p + 1) * LANES)
        qp = q_ref[0, :, cols]
        kk = jnp.concatenate([kp_ref[0, :, cols], kc_ref[0, :, cols]], axis=0)
        vv = jnp.concatenate([vp_ref[0, :, cols], vc_ref[0, :, cols]], axis=0)
        outs, lses = [], []
        for sel in (low, ~low):
            qh = jnp.where(sel, qp, jnp.zeros_like(qp))
            s = _dot_nt(qh, kk)
            s = jnp.where(mask, s, NEG_INF)
            m = jnp.max(s, axis=-1, keepdims=True)
            e = jnp.exp(s - m)
            den = jnp.sum(e, axis=-1, keepdims=True)
            outs.append(_dot(e.astype(jnp.bfloat16), vv) / den)
            lses.append(m + jnp.log(den))
        o_ref[0, :, cols] = jnp.where(low, outs[0], outs[1])
        l_ref[0, :, cols] = jnp.where(low, lses[0], lses[1])


def _attn_prompt(q, kv, b, s, dil):
    sc = s // dil
    nb = sc // QBLK
    qv = q.reshape(b, sc, dil * D_ATTN)
    kvv = kv.reshape(b, sc, dil * 2 * D_ATTN)
    blk = (1, QBLK, D_ATTN)
    cur = lambda off: pl.BlockSpec(blk, lambda bi, r, n: (bi, n, 2 * r + off))
    prev = lambda off: pl.BlockSpec(blk, lambda bi, r, n: (bi, jnp.maximum(n - 1, 0), 2 * r + off))
    qspec = pl.BlockSpec(blk, lambda bi, r, n: (bi, n, r))
    o, l = pl.pallas_call(
        _attn_body,
        grid=(b, dil, nb),
        in_specs=[qspec, cur(0), prev(0), cur(1), prev(1)],
        out_specs=[qspec, qspec],
        out_shape=[jax.ShapeDtypeStruct((b, sc, dil * D_ATTN), jnp.float32)] * 2,
        compiler_params=_params(3),
        name=f"attn_d{dil}",
    )(qv, kvv, kvv, kvv, kvv)
    return o.reshape(b * s, D_ATTN), l.reshape(b * s, D_ATTN)


def _sample_mix_body(q_ref, kv_ref, u_ref, c1_ref, c4_ref, c16_ref, st_ref,
                     attn_ref, z_ref, pool_ref):
    q = q_ref[0]
    kv_new = kv_ref[0].astype(jnp.bfloat16).astype(jnp.float32)
    head_row = lax.broadcasted_iota(jnp.int32, (HEAD_ROWS, D_ATTN), 0)
    head_col = lax.broadcasted_iota(jnp.int32, (HEAD_ROWS, D_ATTN), 1) // HEAD_DIM
    own = head_row == head_col
    qh = jnp.where(own, jnp.broadcast_to(q, (HEAD_ROWS, D_ATTN)), 0.0)
    caches = [c_ref[0].astype(jnp.bfloat16) for c_ref in (c1_ref, c4_ref, c16_ref)]
    k_all = jnp.concatenate([c[:, :D_ATTN] for c in caches], axis=0)
    v_all = jnp.concatenate([c[:, D_ATTN:] for c in caches], axis=0)
    s_hist = _dot_nt(qh.astype(jnp.bfloat16), k_all)
    s_new = jnp.sum(qh * kv_new[:, :D_ATTN], axis=-1, keepdims=True)
    m = jnp.maximum(jnp.max(s_hist, axis=-1, keepdims=True), s_new)
    e_hist = jnp.exp(s_hist - m)
    e_new = float(len(DILATIONS)) * jnp.exp(s_new - m)
    den = jnp.sum(e_hist, axis=-1, keepdims=True) + e_new
    acc = _dot(e_hist.astype(jnp.bfloat16), v_all)
    acc = acc + e_new * kv_new[:, D_ATTN:]
    acc = jnp.where(own, acc / den, 0.0)
    attn_ref[0] = jnp.sum(acc, axis=0, keepdims=True)

    u = u_ref[0]
    st = st_ref[0]
    n_st = st.shape[0]
    gdim = u.shape[1] // len(POOL_WINDOWS)
    col = lax.broadcasted_iota(jnp.int32, u.shape, 1)
    z = jnp.zeros_like(u)
    for gi, w in enumerate(POOL_WINDOWS):
        tot = u + jnp.sum(st[n_st - (w - 1):, :], axis=0, keepdims=True)
        z = jnp.where(col // gdim == gi, tot / float(w) - u, z)
    z_ref[0] = z
    pool_ref[0, :n_st - 1, :] = st[1:, :]
    pool_ref[0, n_st - 1:, :] = u


def _sample_mix(q, kv, u, cache, state, layer):
    nb = q.shape[0]
    _, w_buf, c = cache.shape
    n_st, d_pool = state.shape[1:]
    assert w_buf == SPAN * DILATIONS[-1] and n_st == POOL_WINDOWS[-1] - 1
    views = [cache.reshape(-1, w_buf // d, d * c) for d in DILATIONS]
    one = lambda width: pl.BlockSpec((1, 1, width), lambda i: (i, 0, 0))
    cspec = lambda d: pl.BlockSpec((1, SPAN, c), lambda i: (layer * nb + i, w_buf // (d * SPAN) - 1, 0))
    return pl.pallas_call(
        _sample_mix_body,
        grid=(nb,),
        in_specs=[one(D_ATTN), one(2 * D_ATTN), one(d_pool)] + [cspec(d) for d in DILATIONS]
                 + [pl.BlockSpec((1, n_st, d_pool), lambda i: (layer * nb + i, 0, 0))],
        out_specs=[one(D_ATTN), one(d_pool), pl.BlockSpec((1, n_st, d_pool), lambda i: (i, 0, 0))],
        out_shape=[jax.ShapeDtypeStruct((nb, 1, D_ATTN), jnp.float32),
                   jax.ShapeDtypeStruct((nb, 1, d_pool), jnp.float32),
                   jax.ShapeDtypeStruct((nb, n_st, d_pool), jnp.float32)],
        compiler_params=_params(1),
        name="sample_mix",
    )(q.reshape(nb, 1, D_ATTN), kv.reshape(nb, 1, 2 * D_ATTN), u.reshape(nb, 1, d_pool),
      *views, state)


def _pool_project(x, attn, z, pw_ref, ps_ref, wo_ref):
    gdim = pw_ref.shape[1]
    zb = z.astype(jnp.bfloat16)
    pooled = jnp.concatenate(
        [_dot(zb[:, gi * gdim:(gi + 1) * gdim], pw_ref[gi]) for gi in range(pw_ref.shape[0])], axis=1)
    pooled = pooled * ps_ref[...]
    mix = jnp.concatenate([attn.astype(jnp.bfloat16), pooled.astype(jnp.bfloat16)], axis=1)
    return x + _dot(mix, wo_ref[...])


def _mixout_prompt_body(blocks_per_seq, x_ref, o1_ref, l1_ref, o2_ref, l2_ref, o3_ref, l3_ref,
                        u_ref, halo_ref, pw_ref, ps_ref, wo_ref, h_ref, ubuf):
    tm = x_ref.shape[0]
    l1, l2, l3 = l1_ref[...], l2_ref[...], l3_ref[...]
    m = jnp.maximum(jnp.maximum(l1, l2), l3)
    e1, e2, e3 = jnp.exp(l1 - m), jnp.exp(l2 - m), jnp.exp(l3 - m)
    attn = (e1 * o1_ref[...] + e2 * o2_ref[...] + e3 * o3_ref[...]) / (e1 + e2 + e3)

    blk = pl.program_id(0) % blocks_per_seq
    u = u_ref[...]
    ubuf[:POOL_HALO, :] = jnp.where(blk > 0, halo_ref[...], 0.0)
    ubuf[POOL_HALO:, :] = u
    gdim = u.shape[1] // len(POOL_WINDOWS)
    pos = blk * tm + lax.broadcasted_iota(jnp.int32, (tm, gdim), 0)
    zs = []
    for gi, w in enumerate(POOL_WINDOWS):
        cols = slice(gi * gdim, (gi + 1) * gdim)
        cur = u[:, cols]
        tot = cur
        for k in range(1, w):
            tot = tot + ubuf[POOL_HALO - k:POOL_HALO - k + tm, cols]
        cnt = jnp.minimum(pos + 1, w).astype(jnp.float32)
        zs.append(tot / cnt - cur)
    z = jnp.concatenate(zs, axis=1)
    h_ref[...] = _pool_project(x_ref[...], attn, z, pw_ref, ps_ref, wo_ref)


def _mixout_prompt(x, pats, u, pool_w, pool_scale, w_out, seq, tm):
    n, d = x.shape
    d_pool = u.shape[1]
    row = lambda c: pl.BlockSpec((tm, c), lambda i: (i, 0))
    halo = pl.BlockSpec((POOL_HALO, d_pool), lambda i: (jnp.maximum(i * (tm // POOL_HALO) - 1, 0), 0))
    flat = [a for pair in pats for a in pair]
    return pl.pallas_call(
        functools.partial(_mixout_prompt_body, seq // tm),
        grid=(n // tm,),
        in_specs=[row(d)] + [row(D_ATTN)] * 6 + [row(d_pool), halo,
                  _const_spec(pool_w.shape), _const_spec((1, d_pool)), _const_spec(w_out.shape)],
        out_specs=row(d),
        out_shape=jax.ShapeDtypeStruct((n, d), jnp.float32),
        scratch_shapes=[pltpu.VMEM((tm + POOL_HALO, d_pool), jnp.float32)],
        compiler_params=_params(1),
        name="mixout_prompt",
    )(x, *flat, u, u, pool_w, pool_scale, w_out)


def _mixout_sample_body(x_ref, attn_ref, z_ref, pw_ref, ps_ref, wo_ref, h_ref):
    h_ref[...] = _pool_project(x_ref[...], attn_ref[...], z_ref[...], pw_ref, ps_ref, wo_ref)


def _mixout_sample(x, attn, z, pool_w, pool_scale, w_out):
    n, d = x.shape
    full = lambda a: pl.BlockSpec(a.shape, lambda i: (0,) * a.ndim)
    args = (x, attn, z, pool_w, pool_scale, w_out)
    return pl.pallas_call(
        _mixout_sample_body,
        grid=(1,),
        in_specs=[full(a) for a in args],
        out_specs=pl.BlockSpec((n, d), lambda i: (0, 0)),
        out_shape=jax.ShapeDtypeStruct((n, d), jnp.float32),
        compiler_params=_params(1),
        name="mixout_sample",
    )(*args)


def _mlp_ple_body(ff_chunk, with_final, x_ref, p_ref, gm_ref, wu_ref, wd_ref, gp_ref, wg_ref, wp_ref,
                  gf_ref, h_ref, *y_ref):
    x = x_ref[...]
    xn = _rms(x, gm_ref[...]).astype(jnp.bfloat16)
    h = x
    for c in range(wu_ref.shape[1] // ff_chunk):
        a = jnp.maximum(_dot(xn, wu_ref[:, c * ff_chunk:(c + 1) * ff_chunk]), 0.0)
        h = h + _dot((a * a).astype(jnp.bfloat16), wd_ref[c * ff_chunk:(c + 1) * ff_chunk, :])
    gate = jax.nn.sigmoid(_dot(_rms(h, gp_ref[...]).astype(jnp.bfloat16), wg_ref[...]))
    h = h + _dot(p_ref[...].astype(jnp.bfloat16), wp_ref[...]) * gate
    h_ref[...] = h
    if with_final:
        y_ref[0][...] = _rms(h, gf_ref[...])


def _mlp_ple(x, p, layer, g_mlp, w_up, w_down, g_ple, w_gate, w_ple, g_final, tm, with_final):
    n, d = x.shape
    row = lambda c: pl.BlockSpec((tm, c), lambda i: (i, 0))
    p_spec = pl.BlockSpec((tm, p.shape[1]), lambda i: (layer * (n // tm) + i, 0))
    n_out = 2 if with_final else 1
    outs = pl.pallas_call(
        functools.partial(_mlp_ple_body, min(1024, w_up.shape[1]), with_final),
        grid=(n // tm,),
        in_specs=[row(d), p_spec, _const_spec((1, d)), _const_spec(w_up.shape),
                  _const_spec(w_down.shape), _const_spec((1, d)), _const_spec(w_gate.shape),
                  _const_spec(w_ple.shape), _const_spec((1, d))],
        out_specs=[row(d)] * n_out,
        out_shape=[jax.ShapeDtypeStruct((n, d), jnp.float32)] * n_out,
        compiler_params=_params(1),
        name="mlp_ple",
    )(x, p, g_mlp, w_up, w_down, g_ple, w_gate, w_ple, g_final)
    return outs


def kernel(x_prompt, x_sample, cache_attn_kv, state_pool, p_prompt, p_sample, norm_attn_g, w_in, pool_w,
           pool_scale, w_out, norm_mlp_g, w_up, w_down, ple_norm_g, w_ple_gate, w_ple, final_norm_g):
    b, s, d = x_prompt.shape
    nb_s, t_s, _ = x_sample.shape
    depth = w_in.shape[0]
    d_pool = state_pool.shape[-1]
    n_st = state_pool.shape[2]
    w_buf = cache_attn_kv.shape[2]
    kv_keep = min(SPAN * DILATIONS[-1], s)
    assert t_s == 1 and s % (QBLK * DILATIONS[-1]) == 0
    tm = 512
    bf = lambda a: a.astype(jnp.bfloat16)
    row = lambda a: a.reshape(1, -1)

    hp = x_prompt.reshape(b * s, d)
    hs = x_sample.reshape(nb_s, d)
    g_final = row(final_norm_g)
    pp_all = p_prompt.reshape(depth * b * s, -1)
    ps_all = p_sample.reshape(depth * nb_s, -1)
    cache_all = cache_attn_kv.reshape(depth * nb_s, w_buf, 2 * D_ATTN)
    state_all = state_pool.reshape(depth * nb_s, n_st, d_pool)
    kv_p, kv_s, pool_p, pool_s = [], [], [], []
    yp = ys = None
    for i in range(depth):
        last = i == depth - 1
        wi, wo, wu, wd = bf(w_in[i]), bf(w_out[i]), bf(w_up[i]), bf(w_down[i])
        wg, wp, pw = bf(w_ple_gate[i]), bf(w_ple[i]), bf(pool_w[i])
        ga, gm, gp, ps = row(norm_attn_g[i]), row(norm_mlp_g[i]), row(ple_norm_g[i]), row(pool_scale[i])

        q, kv, kvf, u = _inproj(hp, ga, wi, tm)
        pats = [_attn_prompt(q, kv, b, s, dil) for dil in DILATIONS]
        hp = _mixout_prompt(hp, pats, u, pw, ps, wo, s, tm)
        outs = _mlp_ple(hp, pp_all, i, gm, wu, wd, gp, wg, wp, g_final, tm, last)
        hp = outs[0]
        if last:
            yp = outs[1]
        kv_p.append(kvf.reshape(b, s, 2, N_HEADS, HEAD_DIM)[:, s - kv_keep:])
        pool_p.append(u.reshape(b, s, d_pool)[:, s - n_st:])

        q, kv, kvf, u = _inproj(hs, ga, wi, nb_s)
        attn, z, pool_new = _sample_mix(q.astype(jnp.float32), kvf, u, cache_all, state_all, i)
        hs = _mixout_sample(hs, attn.reshape(nb_s, D_ATTN), z.reshape(nb_s, d_pool), pw, ps, wo)
        outs = _mlp_ple(hs, ps_all, i, gm, wu, wd, gp, wg, wp, g_final, nb_s, last)
        hs = outs[0]
        if last:
            ys = outs[1]
        kv_s.append(kvf.reshape(nb_s, 1, 2, N_HEADS, HEAD_DIM))
        pool_s.append(pool_new)

    return (yp.reshape(b, s, d), ys.reshape(nb_s, 1, d), jnp.stack(kv_p), jnp.stack(kv_s),
            jnp.stack(pool_p), jnp.stack(pool_s))
```

```python
import functools

import jax
import jax.numpy as jnp
from jax import lax
from jax.experimental import pallas as pl
from jax.experimental.pallas import tpu as pltpu

N_HEADS = 8
HEAD_DIM = 64
D_ATTN = N_HEADS * HEAD_DIM
POOL_WINDOWS = (2, 4, 8, 16)
POOL_HALO = 16
DILATIONS = (1, 4, 16)
SPAN = 128
QBLK = 128
LANES = 128
SUBLANES = 8
EPS = 1e-6
NEG_INF = -1e30
ATTN_SCALE = HEAD_DIM ** -0.5
VMEM_LIMIT = 56 * 1024 * 1024


def _rms(x, g):
    return x * lax.rsqrt(jnp.mean(x * x, axis=-1, keepdims=True) + EPS) * g


def _dot(a, b):
    return jnp.dot(a, b, preferred_element_type=jnp.float32)


def _dot_nt(a, b):
    return lax.dot_general(a, b, (((1,), (1,)), ((), ())), preferred_element_type=jnp.float32)


def _const_spec(shape):
    nd = len(shape)
    return pl.BlockSpec(shape, lambda *_: (0,) * nd, pipeline_mode=pl.Buffered(1))


def _params(n_grid):
    return pltpu.CompilerParams(dimension_semantics=("arbitrary",) * n_grid,
                                vmem_limit_bytes=VMEM_LIMIT)


def _project(x_ref, g_ref, w_ref):
    xn = _rms(x_ref[...], g_ref[...]).astype(jnp.bfloat16)
    proj = _dot(xn, w_ref[...])
    return proj[:, :D_ATTN] * ATTN_SCALE, proj[:, D_ATTN:3 * D_ATTN], proj[:, 3 * D_ATTN:]


def _inproj_sample_body(x_ref, g_ref, w_ref, q_ref, kvf_ref, u_ref):
    q, kvf, u = _project(x_ref, g_ref, w_ref)
    q_ref[...] = q
    kvf_ref[...] = kvf
    u_ref[...] = u


def _inproj_sample(x, g, w):
    n, d = x.shape
    d_in = w.shape[1]
    d_pool = d_in - 3 * D_ATTN
    full = lambda r, c: pl.BlockSpec((r, c), lambda i: (0, 0))
    return pl.pallas_call(
        _inproj_sample_body,
        grid=(1,),
        in_specs=[full(n, d), full(1, d), full(d, d_in)],
        out_specs=[full(n, D_ATTN), full(n, 2 * D_ATTN), full(n, d_pool)],
        out_shape=[jax.ShapeDtypeStruct((n, D_ATTN), jnp.float32),
                   jax.ShapeDtypeStruct((n, 2 * D_ATTN), jnp.float32),
                   jax.ShapeDtypeStruct((n, d_pool), jnp.float32)],
        compiler_params=_params(1),
        name="inproj_sample",
    )(x, g, w)


def _inproj_prompt_body(first_kept, x_ref, g_ref, w_ref, *refs):
    n_pat = len(DILATIONS)
    q_refs, kv_refs = refs[:n_pat], refs[n_pat:2 * n_pat]
    kvt_ref, u_ref, buf = refs[2 * n_pat:]
    tm = x_ref.shape[0]
    q, kvf, u = _project(x_ref, g_ref, w_ref)
    u_ref[...] = u
    n_q = D_ATTN // LANES
    for c in range(buf.shape[0]):
        src, c0 = (q, c) if c < n_q else (kvf, c - n_q)
        buf[c] = src[:, c0 * LANES:(c0 + 1) * LANES]
    for dil, q_ref, kv_ref in zip(DILATIONS, q_refs, kv_refs):
        for r in range(dil):
            rows = slice(None) if dil == 1 else pl.ds(r, tm // dil, stride=dil)
            for c in range(buf.shape[0]):
                dst, c0 = (q_ref, c) if c < n_q else (kv_ref, c - n_q)
                dst[0, r, :, c0 * LANES:(c0 + 1) * LANES] = buf[c, rows, :].astype(jnp.bfloat16)

    @pl.when(pl.program_id(1) >= first_kept)
    def _():
        kvt_ref[0] = kvf.T


def _inproj_prompt(x, g, w, tm, seq, keep):
    n, d = x.shape
    d_in = w.shape[1]
    d_pool = d_in - 3 * D_ATTN
    b, bps = n // seq, seq // tm
    first_kept = (seq - keep) // tm
    row = lambda c: pl.BlockSpec((tm, c), lambda bi, j: (bi * bps + j, 0))
    cls = lambda dil, c: pl.BlockSpec((1, dil, tm // dil, c), lambda bi, j: (bi, 0, j, 0))
    cls_shape = lambda dil, c: jax.ShapeDtypeStruct((b, dil, seq // dil, c), jnp.bfloat16)
    kvt_spec = pl.BlockSpec((1, 2 * D_ATTN, tm), lambda bi, j: (bi, 0, jnp.maximum(j - first_kept, 0)))
    n_pat = len(DILATIONS)
    outs = pl.pallas_call(
        functools.partial(_inproj_prompt_body, first_kept),
        grid=(b, bps),
        in_specs=[row(d), _const_spec((1, d)), _const_spec((d, d_in))],
        out_specs=[cls(dil, D_ATTN) for dil in DILATIONS] + [cls(dil, 2 * D_ATTN) for dil in DILATIONS]
                  + [kvt_spec, row(d_pool)],
        out_shape=[cls_shape(dil, D_ATTN) for dil in DILATIONS]
                  + [cls_shape(dil, 2 * D_ATTN) for dil in DILATIONS]
                  + [jax.ShapeDtypeStruct((b, 2 * D_ATTN, keep), jnp.float32),
                     jax.ShapeDtypeStruct((n, d_pool), jnp.float32)],
        scratch_shapes=[pltpu.VMEM((3 * D_ATTN // LANES, tm, LANES), jnp.float32)],
        compiler_params=_params(2),
        name="inproj_prompt",
    )(x, g, w)
    return outs[:n_pat], outs[n_pat:2 * n_pat], outs[2 * n_pat], outs[2 * n_pat + 1]


def _attn_body(dil, q_ref, kc_ref, kp_ref, o_ref, l_ref):
    n = pl.program_id(1)
    qi = lax.broadcasted_iota(jnp.int32, (QBLK, 2 * QBLK), 0)
    kj = lax.broadcasted_iota(jnp.int32, (QBLK, 2 * QBLK), 1) - QBLK
    rel = qi - kj
    mask = (rel >= 0) & (rel <= SPAN) & ((kj >= 0) | (n > 0))
    lane = lax.broadcasted_iota(jnp.int32, (QBLK, LANES), 1)
    low = lane < HEAD_DIM

    def one_class(r, carry):
        rows = slice(None) if dil == 1 else pl.ds(r, QBLK, stride=dil)
        for p in range(D_ATTN // LANES):
            cols = slice(p * LANES, (p + 1) * LANES)
            vcols = slice(D_ATTN + p * LANES, D_ATTN + (p + 1) * LANES)
            qp = q_ref[0, r, :, cols]
            kk = jnp.concatenate([kp_ref[0, r, :, cols], kc_ref[0, r, :, cols]], axis=0)
            vv = jnp.concatenate([kp_ref[0, r, :, vcols], kc_ref[0, r, :, vcols]], axis=0)
            outs, lses = [], []
            for sel in (low, ~low):
                qh = jnp.where(sel, qp, jnp.zeros_like(qp))
                s = _dot_nt(qh, kk)
                s = jnp.where(mask, s, NEG_INF)
                m = jnp.max(s, axis=-1, keepdims=True)
                e = jnp.exp(s - m)
                den = jnp.sum(e, axis=-1, keepdims=True)
                outs.append(_dot(e.astype(jnp.bfloat16), vv) / den)
                lses.append(m + jnp.log(den))
            o_ref[0, p, rows, :] = jnp.where(low, outs[0], outs[1])
            l_ref[0, p, rows, :] = jnp.where(low, lses[0], lses[1])
        return carry

    if dil == 1:
        one_class(0, None)
    else:
        lax.fori_loop(0, dil, one_class, None)


def _attn_prompt(q, kv, dil):
    b, _, sc, _ = q.shape
    nb = sc // QBLK
    cur = lambda c: pl.BlockSpec((1, dil, QBLK, c), lambda bi, n: (bi, 0, n, 0))
    prev = pl.BlockSpec((1, dil, QBLK, 2 * D_ATTN), lambda bi, n: (bi, 0, jnp.maximum(n - 1, 0), 0))
    n_pair = D_ATTN // LANES
    out = pl.BlockSpec((1, n_pair, QBLK * dil, LANES), lambda bi, n: (bi, 0, n, 0))
    return pl.pallas_call(
        functools.partial(_attn_body, dil),
        grid=(b, nb),
        in_specs=[cur(D_ATTN), cur(2 * D_ATTN), prev],
        out_specs=[out, out],
        out_shape=[jax.ShapeDtypeStruct((b, n_pair, sc * dil, LANES), jnp.float32)] * 2,
        compiler_params=_params(2),
        name=f"attn_d{dil}",
    )(q, kv, kv)


def _sample_attn_body(q_ref, kvn_ref, c_ref, o_ref):
    w_buf = c_ref.shape[-1]
    pos = lax.broadcasted_iota(jnp.int32, (1, w_buf), 1)
    dist = w_buf - pos
    mult = jnp.zeros((1, w_buf), jnp.float32)
    for dil in DILATIONS:
        mult = mult + ((dist % dil == 0) & (dist <= SPAN * dil)).astype(jnp.float32)
    rows, news = [], []
    for h in range(N_HEADS):
        acc = None
        for g in range(HEAD_DIM // SUBLANES):
            ch = slice(h * HEAD_DIM + g * SUBLANES, h * HEAD_DIM + (g + 1) * SUBLANES)
            part = c_ref[0, 0, ch, :] * q_ref[0, ch, :]
            acc = part if acc is None else acc + part
        rows.append(jnp.sum(acc, axis=0, keepdims=True))
        hs = slice(h * HEAD_DIM, (h + 1) * HEAD_DIM)
        news.append(jnp.sum(q_ref[0, hs, :] * kvn_ref[0, hs, :], axis=0, keepdims=True))
    s = jnp.where(mult > 0.0, jnp.concatenate(rows, axis=0), NEG_INF)
    s_new = jnp.concatenate(news, axis=0)
    m = jnp.maximum(jnp.max(s, axis=-1, keepdims=True), s_new)
    p = mult * jnp.exp(s - m)
    p_new = float(len(DILATIONS)) * jnp.exp(s_new - m)
    den = jnp.sum(p, axis=-1, keepdims=True) + p_new
    for h in range(N_HEADS):
        hs = slice(h * HEAD_DIM, (h + 1) * HEAD_DIM)
        vs = slice(D_ATTN + h * HEAD_DIM, D_ATTN + (h + 1) * HEAD_DIM)
        pv = jnp.sum(c_ref[0, 1, hs, :] * p[h:h + 1, :], axis=1, keepdims=True)
        o_ref[0, hs, :] = (pv + p_new[h:h + 1, :] * kvn_ref[0, vs, :]) / den[h:h + 1, :]


def _sample_attn(q_col, kv_col, cache_t, layer):
    nb = q_col.shape[0]
    w_buf = cache_t.shape[-1]
    assert w_buf == SPAN * DILATIONS[-1]
    col = lambda c: pl.BlockSpec((1, c, 1), lambda i: (i, 0, 0))
    return pl.pallas_call(
        _sample_attn_body,
        grid=(nb,),
        in_specs=[col(D_ATTN), col(2 * D_ATTN),
                  pl.BlockSpec((1, 2, D_ATTN, w_buf), lambda i: (layer * nb + i, 0, 0, 0))],
        out_specs=col(D_ATTN),
        out_shape=jax.ShapeDtypeStruct((nb, D_ATTN, 1), jnp.float32),
        compiler_params=_params(1),
        name="sample_attn",
    )(q_col, kv_col, cache_t)


def _pool_project(x, attn, z, pw_ref, ps_ref, wo_ref):
    gdim = pw_ref.shape[1]
    zb = z.astype(jnp.bfloat16)
    pooled = jnp.concatenate(
        [_dot(zb[:, gi * gdim:(gi + 1) * gdim], pw_ref[gi]) for gi in range(pw_ref.shape[0])], axis=1)
    pooled = pooled * ps_ref[...]
    mix = jnp.concatenate([attn.astype(jnp.bfloat16), pooled.astype(jnp.bfloat16)], axis=1)
    return x + _dot(mix, wo_ref[...])


def _mixout_prompt_body(blocks_per_seq, x_ref, o1_ref, l1_ref, o2_ref, l2_ref, o3_ref, l3_ref,
                        u_ref, halo_ref, pw_ref, ps_ref, wo_ref, h_ref, ubuf):
    tm = x_ref.shape[0]
    pairs = []
    for p in range(o1_ref.shape[1]):
        l1, l2, l3 = l1_ref[0, p], l2_ref[0, p], l3_ref[0, p]
        m = jnp.maximum(jnp.maximum(l1, l2), l3)
        e1, e2, e3 = jnp.exp(l1 - m), jnp.exp(l2 - m), jnp.exp(l3 - m)
        pairs.append((e1 * o1_ref[0, p] + e2 * o2_ref[0, p] + e3 * o3_ref[0, p]) / (e1 + e2 + e3))
    attn = jnp.concatenate(pairs, axis=1)

    blk = pl.program_id(0) % blocks_per_seq
    u = u_ref[...]
    ubuf[:POOL_HALO, :] = jnp.where(blk > 0, halo_ref[...], 0.0)
    ubuf[POOL_HALO:, :] = u
    gdim = u.shape[1] // len(POOL_WINDOWS)
    pos = blk * tm + lax.broadcasted_iota(jnp.int32, (tm, gdim), 0)
    zs = []
    for gi, w in enumerate(POOL_WINDOWS):
        cols = slice(gi * gdim, (gi + 1) * gdim)
        cur = u[:, cols]
        tot = cur
        for k in range(1, w):
            tot = tot + ubuf[POOL_HALO - k:POOL_HALO - k + tm, cols]
        cnt = jnp.minimum(pos + 1, w).astype(jnp.float32)
        zs.append(tot / cnt - cur)
    z = jnp.concatenate(zs, axis=1)
    h_ref[...] = _pool_project(x_ref[...], attn, z, pw_ref, ps_ref, wo_ref)


def _mixout_prompt(x, pats, u, pool_w, pool_scale, w_out, seq, tm):
    n, d = x.shape
    d_pool = u.shape[1]
    row = lambda c: pl.BlockSpec((tm, c), lambda i: (i, 0))
    halo = pl.BlockSpec((POOL_HALO, d_pool), lambda i: (jnp.maximum(i * (tm // POOL_HALO) - 1, 0), 0))
    flat = [a for pair in pats for a in pair]
    bps = seq // tm
    pat = pl.BlockSpec((1, D_ATTN // LANES, tm, LANES), lambda i: (i // bps, 0, i % bps, 0))
    return pl.pallas_call(
        functools.partial(_mixout_prompt_body, bps),
        grid=(n // tm,),
        in_specs=[row(d)] + [pat] * 6 + [row(d_pool), halo,
                  _const_spec(pool_w.shape), _const_spec((1, d_pool)), _const_spec(w_out.shape)],
        out_specs=row(d),
        out_shape=jax.ShapeDtypeStruct((n, d), jnp.float32),
        scratch_shapes=[pltpu.VMEM((tm + POOL_HALO, d_pool), jnp.float32)],
        compiler_params=_params(1),
        name="mixout_prompt",
    )(x, *flat, u, u, pool_w, pool_scale, w_out)


def _mixout_sample_body(x_ref, attn_ref, u_ref, st_ref, pw_ref, ps_ref, wo_ref, h_ref, pool_ref):
    u = u_ref[...]
    n_st = st_ref.shape[1]
    gdim = u.shape[1] // len(POOL_WINDOWS)
    zs = []
    for gi, w in enumerate(POOL_WINDOWS):
        cols = slice(gi * gdim, (gi + 1) * gdim)
        cur = u[:, cols]
        tot = cur
        for k in range(1, w):
            tot = tot + st_ref[0, n_st - k, :, cols]
        zs.append(tot / float(w) - cur)
    z = jnp.concatenate(zs, axis=1)
    h_ref[...] = _pool_project(x_ref[...], attn_ref[...], z, pw_ref, ps_ref, wo_ref)
    pool_ref[:n_st - 1] = st_ref[0, 1:]
    pool_ref[n_st - 1] = u


def _mixout_sample(x, attn, u, state_t, layer, pool_w, pool_scale, w_out):
    n, d = x.shape
    _, n_st, _, d_pool = state_t.shape
    assert n_st == POOL_WINDOWS[-1] - 1
    full = lambda a: pl.BlockSpec(a.shape, lambda i: (0,) * a.ndim)
    st_spec = pl.BlockSpec((1, n_st, n, d_pool), lambda i: (layer, 0, 0, 0))
    return pl.pallas_call(
        _mixout_sample_body,
        grid=(1,),
        in_specs=[full(x), full(attn), full(u), st_spec, full(pool_w), full(pool_scale), full(w_out)],
        out_specs=[pl.BlockSpec((n, d), lambda i: (0, 0)), pl.BlockSpec((n_st, n, d_pool), lambda i: (0, 0, 0))],
        out_shape=[jax.ShapeDtypeStruct((n, d), jnp.float32),
                   jax.ShapeDtypeStruct((n_st, n, d_pool), jnp.float32)],
        compiler_params=_params(1),
        name="mixout_sample",
    )(x, attn, u, state_t, pool_w, pool_scale, w_out)


def _mlp_ple_body(ff_chunk, with_final, x_ref, p_ref, gm_ref, wu_ref, wd_ref, gp_ref, wg_ref, wp_ref,
                  gf_ref, h_ref, *y_ref):
    x = x_ref[...]
    xn = _rms(x, gm_ref[...]).astype(jnp.bfloat16)
    h = x
    for c in range(wu_ref.shape[1] // ff_chunk):
        a = jnp.maximum(_dot(xn, wu_ref[:, c * ff_chunk:(c + 1) * ff_chunk]), 0.0)
        h = h + _dot((a * a).astype(jnp.bfloat16), wd_ref[c * ff_chunk:(c + 1) * ff_chunk, :])
    gate = jax.nn.sigmoid(_dot(_rms(h, gp_ref[...]).astype(jnp.bfloat16), wg_ref[...]))
    h = h + _dot(p_ref[...].astype(jnp.bfloat16), wp_ref[...]) * gate
    h_ref[...] = h
    if with_final:
        y_ref[0][...] = _rms(h, gf_ref[...])


def _mlp_ple(x, p, layer, g_mlp, w_up, w_down, g_ple, w_gate, w_ple, g_final, tm, with_final):
    n, d = x.shape
    row = lambda c: pl.BlockSpec((tm, c), lambda i: (i, 0))
    p_spec = pl.BlockSpec((tm, p.shape[1]), lambda i: (layer * (n // tm) + i, 0))
    n_out = 2 if with_final else 1
    outs = pl.pallas_call(
        functools.partial(_mlp_ple_body, min(1024, w_up.shape[1]), with_final),
        grid=(n // tm,),
        in_specs=[row(d), p_spec, _const_spec((1, d)), _const_spec(w_up.shape),
                  _const_spec(w_down.shape), _const_spec((1, d)), _const_spec(w_gate.shape),
                  _const_spec(w_ple.shape), _const_spec((1, d))],
        out_specs=[row(d)] * n_out,
        out_shape=[jax.ShapeDtypeStruct((n, d), jnp.float32)] * n_out,
        compiler_params=_params(1),
        name="mlp_ple",
    )(x, p, g_mlp, w_up, w_down, g_ple, w_gate, w_ple, g_final)
    return outs


def kernel(x_prompt, x_sample, cache_attn_kv, state_pool, p_prompt, p_sample, norm_attn_g, w_in, pool_w,
           pool_scale, w_out, norm_mlp_g, w_up, w_down, ple_norm_g, w_ple_gate, w_ple, final_norm_g):
    b, s, d = x_prompt.shape
    nb_s, t_s, _ = x_sample.shape
    depth = w_in.shape[0]
    d_pool = state_pool.shape[-1]
    n_st = state_pool.shape[2]
    w_buf = cache_attn_kv.shape[2]
    kv_keep = min(SPAN * DILATIONS[-1], s)
    assert t_s == 1 and s % (QBLK * DILATIONS[-1]) == 0
    tm = 512
    bf = lambda a: a.astype(jnp.bfloat16)
    row = lambda a: a.reshape(1, -1)

    hp = x_prompt.reshape(b * s, d)
    hs = x_sample.reshape(nb_s, d)
    g_final = row(final_norm_g)
    pp_all = p_prompt.reshape(depth * b * s, -1)
    ps_all = p_sample.reshape(depth * nb_s, -1)
    cache_t = cache_attn_kv.transpose(0, 1, 3, 4, 5, 2).reshape(depth * nb_s, 2, D_ATTN, w_buf)
    state_t = state_pool.transpose(0, 2, 1, 3)
    kv_p, kv_s, pool_p, pool_s = [], [], [], []
    yp = ys = None
    for i in range(depth):
        last = i == depth - 1
        wi, wo, wu, wd = bf(w_in[i]), bf(w_out[i]), bf(w_up[i]), bf(w_down[i])
        wg, wp, pw = bf(w_ple_gate[i]), bf(w_ple[i]), bf(pool_w[i])
        ga, gm, gp, ps = row(norm_attn_g[i]), row(norm_mlp_g[i]), row(ple_norm_g[i]), row(pool_scale[i])

        qs, kvs, kvt, u = _inproj_prompt(hp, ga, wi, tm, s, kv_keep)
        pats = [_attn_prompt(q, kv, dil) for q, kv, dil in zip(qs, kvs, DILATIONS)]
        hp = _mixout_prompt(hp, pats, u, pw, ps, wo, s, tm)
        outs = _mlp_ple(hp, pp_all, i, gm, wu, wd, gp, wg, wp, g_final, tm, last)
        hp = outs[0]
        if last:
            yp = outs[1]
        kv_p.append(kvt)
        pool_p.append(u.reshape(b, s, d_pool)[:, s - n_st:])

        q, kvf, u = _inproj_sample(hs, ga, wi)
        attn = _sample_attn(q.reshape(nb_s, D_ATTN, 1), kvf.reshape(nb_s, 2 * D_ATTN, 1), cache_t, i)
        hs, pool_new = _mixout_sample(hs, attn.reshape(nb_s, D_ATTN), u, state_t, i, pw, ps, wo)
        outs = _mlp_ple(hs, ps_all, i, gm, wu, wd, gp, wg, wp, g_final, nb_s, last)
        hs = outs[0]
        if last:
            ys = outs[1]
        kv_s.append(kvf.reshape(nb_s, 1, 2, N_HEADS, HEAD_DIM))
        pool_s.append(pool_new)

    kv_prompt = jnp.stack(kv_p).reshape(depth, b, 2, N_HEADS, HEAD_DIM, kv_keep).transpose(0, 1, 5, 2, 3, 4)
    pool_sample = jnp.stack(pool_s).transpose(0, 2, 1, 3)
    return (yp.reshape(b, s, d), ys.reshape(nb_s, 1, d), kv_prompt, jnp.stack(kv_s),
            jnp.stack(pool_p), pool_sample)
```

```python
import functools

import jax
import jax.numpy as jnp
from jax import lax
from jax.experimental import pallas as pl
from jax.experimental.pallas import tpu as pltpu

N_HEADS = 8
HEAD_DIM = 64
D_ATTN = N_HEADS * HEAD_DIM
POOL_WINDOWS = (2, 4, 8, 16)
POOL_HALO = 16
DILATIONS = (1, 4, 16)
SPAN = 128
QBLK = 128
ATTN_UNROLL = 4
LANES = 128
SUBLANES = 8
EPS = 1e-6
NEG_INF = -1e30
ATTN_SCALE = HEAD_DIM ** -0.5
LOG2E = 1.4426950408889634
LN2 = 0.6931471805599453
VMEM_LIMIT = 56 * 1024 * 1024


def _rms(x, g):
    return x * lax.rsqrt(jnp.mean(x * x, axis=-1, keepdims=True) + EPS) * g


def _dot(a, b):
    return jnp.dot(a, b, preferred_element_type=jnp.float32)


def _dot_nt(a, b):
    return lax.dot_general(a, b, (((1,), (1,)), ((), ())), preferred_element_type=jnp.float32)


def _const_spec(shape):
    nd = len(shape)
    return pl.BlockSpec(shape, lambda *_: (0,) * nd, pipeline_mode=pl.Buffered(1))


def _params(n_grid):
    return pltpu.CompilerParams(dimension_semantics=("arbitrary",) * n_grid,
                                vmem_limit_bytes=VMEM_LIMIT)


def _project(x_ref, g_ref, w_ref, q_scale):
    xn = _rms(x_ref[...], g_ref[...]).astype(jnp.bfloat16)
    proj = _dot(xn, w_ref[...])
    return proj[:, :D_ATTN] * q_scale, proj[:, D_ATTN:3 * D_ATTN], proj[:, 3 * D_ATTN:]


def _inproj_sample_body(x_ref, g_ref, w_ref, q_ref, kvf_ref, u_ref):
    q, kvf, u = _project(x_ref, g_ref, w_ref, ATTN_SCALE)
    q_ref[...] = q
    kvf_ref[...] = kvf
    u_ref[...] = u


def _inproj_sample(x, g, w):
    n, d = x.shape
    d_in = w.shape[1]
    d_pool = d_in - 3 * D_ATTN
    full = lambda r, c: pl.BlockSpec((r, c), lambda i: (0, 0))
    return pl.pallas_call(
        _inproj_sample_body,
        grid=(1,),
        in_specs=[full(n, d), full(1, d), full(d, d_in)],
        out_specs=[full(n, D_ATTN), full(n, 2 * D_ATTN), full(n, d_pool)],
        out_shape=[jax.ShapeDtypeStruct((n, D_ATTN), jnp.float32),
                   jax.ShapeDtypeStruct((n, 2 * D_ATTN), jnp.float32),
                   jax.ShapeDtypeStruct((n, d_pool), jnp.float32)],
        compiler_params=_params(1),
        name="inproj_sample",
    )(x, g, w)


def _inproj_prompt_body(first_kept, x_ref, g_ref, w_ref, *refs):
    n_pat = len(DILATIONS)
    q_refs, kv_refs = refs[:n_pat], refs[n_pat:2 * n_pat]
    kvt_ref, u_ref, buf = refs[2 * n_pat:]
    tm = x_ref.shape[0]
    q, kvf, u = _project(x_ref, g_ref, w_ref, ATTN_SCALE * LOG2E)
    u_ref[...] = u
    n_q = D_ATTN // LANES
    for c in range(buf.shape[0]):
        src, c0 = (q, c) if c < n_q else (kvf, c - n_q)
        buf[c] = src[:, c0 * LANES:(c0 + 1) * LANES]
    for dil, q_ref, kv_ref in zip(DILATIONS, q_refs, kv_refs):
        for r in range(dil):
            rows = slice(None) if dil == 1 else pl.ds(r, tm // dil, stride=dil)
            for c in range(buf.shape[0]):
                dst, c0 = (q_ref, c) if c < n_q else (kv_ref, c - n_q)
                dst[0, r, :, c0 * LANES:(c0 + 1) * LANES] = buf[c, rows, :].astype(jnp.bfloat16)

    @pl.when(pl.program_id(1) >= first_kept)
    def _():
        kvt_ref[0] = kvf.T


def _inproj_prompt(x, g, w, tm, seq, keep):
    n, d = x.shape
    d_in = w.shape[1]
    d_pool = d_in - 3 * D_ATTN
    b, bps = n // seq, seq // tm
    first_kept = (seq - keep) // tm
    row = lambda c: pl.BlockSpec((tm, c), lambda bi, j: (bi * bps + j, 0))
    cls = lambda dil, c: pl.BlockSpec((1, dil, tm // dil, c), lambda bi, j: (bi, 0, j, 0))
    cls_shape = lambda dil, c: jax.ShapeDtypeStruct((b, dil, seq // dil, c), jnp.bfloat16)
    kvt_spec = pl.BlockSpec((1, 2 * D_ATTN, tm), lambda bi, j: (bi, 0, jnp.maximum(j - first_kept, 0)))
    n_pat = len(DILATIONS)
    outs = pl.pallas_call(
        functools.partial(_inproj_prompt_body, first_kept),
        grid=(b, bps),
        in_specs=[row(d), _const_spec((1, d)), _const_spec((d, d_in))],
        out_specs=[cls(dil, D_ATTN) for dil in DILATIONS] + [cls(dil, 2 * D_ATTN) for dil in DILATIONS]
                  + [kvt_spec, row(d_pool)],
        out_shape=[cls_shape(dil, D_ATTN) for dil in DILATIONS]
                  + [cls_shape(dil, 2 * D_ATTN) for dil in DILATIONS]
                  + [jax.ShapeDtypeStruct((b, 2 * D_ATTN, keep), jnp.float32),
                     jax.ShapeDtypeStruct((n, d_pool), jnp.float32)],
        scratch_shapes=[pltpu.VMEM((3 * D_ATTN // LANES, tm, LANES), jnp.float32)],
        compiler_params=_params(2),
        name="inproj_prompt",
    )(x, g, w)
    return outs[:n_pat], outs[n_pat:2 * n_pat], outs[2 * n_pat], outs[2 * n_pat + 1]


def _attn_body(dil, rblocks, q_ref, kc_ref, kp_ref, o_ref, l_ref):
    n = pl.program_id(1)
    qi = lax.broadcasted_iota(jnp.int32, (QBLK, 2 * QBLK), 0)
    kj = lax.broadcasted_iota(jnp.int32, (QBLK, 2 * QBLK), 1) - QBLK
    rel = qi - kj
    band = (rel >= 0) & (rel <= SPAN)
    band_first = band & ((kj >= 0) | (n > 0))
    lane = lax.broadcasted_iota(jnp.int32, (QBLK, LANES), 1)
    low = lane < HEAD_DIM

    def one_block(r, j):
        mask = band_first if j == 0 else band
        q_rows = slice(j * QBLK, (j + 1) * QBLK)
        start = j * QBLK * dil + r
        rows = pl.ds(start, QBLK) if dil == 1 else pl.ds(start, QBLK, stride=dil)
        for p in range(D_ATTN // LANES):
            cols = slice(p * LANES, (p + 1) * LANES)
            vcols = slice(D_ATTN + p * LANES, D_ATTN + (p + 1) * LANES)
            qp = q_ref[0, r, q_rows, cols]
            if j == 0:
                k_prev, v_prev = kp_ref[0, r, :, cols], kp_ref[0, r, :, vcols]
            else:
                p_rows = slice((j - 1) * QBLK, j * QBLK)
                k_prev, v_prev = kc_ref[0, r, p_rows, cols], kc_ref[0, r, p_rows, vcols]
            kk = jnp.concatenate([k_prev, kc_ref[0, r, q_rows, cols]], axis=0)
            vv = jnp.concatenate([v_prev, kc_ref[0, r, q_rows, vcols]], axis=0)
            accs, ms, dens = [], [], []
            for sel in (low, ~low):
                qh = jnp.where(sel, qp, jnp.zeros_like(qp))
                s = _dot_nt(qh, kk)
                s = jnp.where(mask, s, NEG_INF)
                m = jnp.max(s, axis=-1, keepdims=True)
                e = jnp.exp2(s - m)
                ms.append(m)
                dens.append(jnp.sum(e, axis=-1, keepdims=True))
                accs.append(_dot(e.astype(jnp.bfloat16), vv))
            den = jnp.where(low, dens[0], dens[1])
            o_ref[0, p, rows, :] = jnp.where(low, accs[0], accs[1]) / den
            l_ref[0, p, rows, :] = (jnp.where(low, ms[0], ms[1]) + jnp.log2(den)) * LN2

    if dil <= ATTN_UNROLL:
        for r in range(dil):
            for j in range(rblocks):
                one_block(r, j)
    else:
        def classes(i, carry):
            for k in range(ATTN_UNROLL):
                for j in range(rblocks):
                    one_block(i * ATTN_UNROLL + k, j)
            return carry
        lax.fori_loop(0, dil // ATTN_UNROLL, classes, None)


def _attn_prompt(q, kv, dil):
    b, _, sc, _ = q.shape
    rblocks = max(ATTN_UNROLL // dil, 1)
    rows = QBLK * rblocks
    nb = sc // rows
    cur = lambda c: pl.BlockSpec((1, dil, rows, c), lambda bi, n: (bi, 0, n, 0))
    prev = pl.BlockSpec((1, dil, QBLK, 2 * D_ATTN), lambda bi, n: (bi, 0, jnp.maximum(n * rblocks - 1, 0), 0))
    n_pair = D_ATTN // LANES
    out = pl.BlockSpec((1, n_pair, rows * dil, LANES), lambda bi, n: (bi, 0, n, 0))
    return pl.pallas_call(
        functools.partial(_attn_body, dil, rblocks),
        grid=(b, nb),
        in_specs=[cur(D_ATTN), cur(2 * D_ATTN), prev],
        out_specs=[out, out],
        out_shape=[jax.ShapeDtypeStruct((b, n_pair, sc * dil, LANES), jnp.float32)] * 2,
        compiler_params=_params(2),
        name=f"attn_d{dil}",
    )(q, kv, kv)


def _sample_attn_body(q_ref, kvn_ref, c_ref, o_ref):
    w_buf = c_ref.shape[-1]
    pos = lax.broadcasted_iota(jnp.int32, (1, w_buf), 1)
    dist = w_buf - pos
    mult = jnp.zeros((1, w_buf), jnp.float32)
    for dil in DILATIONS:
        mult = mult + ((dist % dil == 0) & (dist <= SPAN * dil)).astype(jnp.float32)
    rows, news = [], []
    for h in range(N_HEADS):
        acc = None
        for g in range(HEAD_DIM // SUBLANES):
            ch = slice(h * HEAD_DIM + g * SUBLANES, h * HEAD_DIM + (g + 1) * SUBLANES)
            part = c_ref[0, 0, ch, :] * q_ref[0, ch, :]
            acc = part if acc is None else acc + part
        rows.append(jnp.sum(acc, axis=0, keepdims=True))
        hs = slice(h * HEAD_DIM, (h + 1) * HEAD_DIM)
        news.append(jnp.sum(q_ref[0, hs, :] * kvn_ref[0, hs, :], axis=0, keepdims=True))
    s = jnp.where(mult > 0.0, jnp.concatenate(rows, axis=0), NEG_INF)
    s_new = jnp.concatenate(news, axis=0)
    m = jnp.maximum(jnp.max(s, axis=-1, keepdims=True), s_new)
    p = mult * jnp.exp(s - m)
    p_new = float(len(DILATIONS)) * jnp.exp(s_new - m)
    den = jnp.sum(p, axis=-1, keepdims=True) + p_new
    for h in range(N_HEADS):
        hs = slice(h * HEAD_DIM, (h + 1) * HEAD_DIM)
        vs = slice(D_ATTN + h * HEAD_DIM, D_ATTN + (h + 1) * HEAD_DIM)
        pv = jnp.sum(c_ref[0, 1, hs, :] * p[h:h + 1, :], axis=1, keepdims=True)
        o_ref[0, hs, :] = (pv + p_new[h:h + 1, :] * kvn_ref[0, vs, :]) / den[h:h + 1, :]


def _sample_attn(q_col, kv_col, cache_t, layer):
    nb = q_col.shape[0]
    w_buf = cache_t.shape[-1]
    assert w_buf == SPAN * DILATIONS[-1]
    col = lambda c: pl.BlockSpec((1, c, 1), lambda i: (i, 0, 0))
    return pl.pallas_call(
        _sample_attn_body,
        grid=(nb,),
        in_specs=[col(D_ATTN), col(2 * D_ATTN),
                  pl.BlockSpec((1, 2, D_ATTN, w_buf), lambda i: (layer * nb + i, 0, 0, 0))],
        out_specs=col(D_ATTN),
        out_shape=jax.ShapeDtypeStruct((nb, D_ATTN, 1), jnp.float32),
        compiler_params=_params(1),
        name="sample_attn",
    )(q_col, kv_col, cache_t)


def _pool_project(x, attn, z, pw_ref, ps_ref, wo_ref):
    gdim = pw_ref.shape[1]
    zb = z.astype(jnp.bfloat16)
    pooled = jnp.concatenate(
        [_dot(zb[:, gi * gdim:(gi + 1) * gdim], pw_ref[gi]) for gi in range(pw_ref.shape[0])], axis=1)
    pooled = pooled * ps_ref[...]
    mix = jnp.concatenate([attn.astype(jnp.bfloat16), pooled.astype(jnp.bfloat16)], axis=1)
    return x + _dot(mix, wo_ref[...])


def _mixout_prompt_body(blocks_per_seq, x_ref, o1_ref, l1_ref, o2_ref, l2_ref, o3_ref, l3_ref,
                        u_ref, halo_ref, pw_ref, ps_ref, wo_ref, h_ref, ubuf):
    tm = x_ref.shape[0]
    pairs = []
    for p in range(o1_ref.shape[1]):
        l1, l2, l3 = l1_ref[0, p], l2_ref[0, p], l3_ref[0, p]
        m = jnp.maximum(jnp.maximum(l1, l2), l3)
        e1, e2, e3 = jnp.exp(l1 - m), jnp.exp(l2 - m), jnp.exp(l3 - m)
        pairs.append((e1 * o1_ref[0, p] + e2 * o2_ref[0, p] + e3 * o3_ref[0, p]) / (e1 + e2 + e3))
    attn = jnp.concatenate(pairs, axis=1)

    blk = pl.program_id(0) % blocks_per_seq
    u = u_ref[...]
    ubuf[:POOL_HALO, :] = jnp.where(blk > 0, halo_ref[...], 0.0)
    ubuf[POOL_HALO:, :] = u
    gdim = u.shape[1] // len(POOL_WINDOWS)
    pos = blk * tm + lax.broadcasted_iota(jnp.int32, (tm, gdim), 0)
    zs = []
    for gi, w in enumerate(POOL_WINDOWS):
        cols = slice(gi * gdim, (gi + 1) * gdim)
        cur = u[:, cols]
        tot = cur
        for k in range(1, w):
            tot = tot + ubuf[POOL_HALO - k:POOL_HALO - k + tm, cols]
        cnt = jnp.minimum(pos + 1, w).astype(jnp.float32)
        zs.append(tot / cnt - cur)
    z = jnp.concatenate(zs, axis=1)
    h_ref[...] = _pool_project(x_ref[...], attn, z, pw_ref, ps_ref, wo_ref)


def _mixout_prompt(x, pats, u, pool_w, pool_scale, w_out, seq, tm):
    n, d = x.shape
    d_pool = u.shape[1]
    row = lambda c: pl.BlockSpec((tm, c), lambda i: (i, 0))
    halo = pl.BlockSpec((POOL_HALO, d_pool), lambda i: (jnp.maximum(i * (tm // POOL_HALO) - 1, 0), 0))
    flat = [a for pair in pats for a in pair]
    bps = seq // tm
    pat = pl.BlockSpec((1, D_ATTN // LANES, tm, LANES), lambda i: (i // bps, 0, i % bps, 0))
    return pl.pallas_call(
        functools.partial(_mixout_prompt_body, bps),
        grid=(n // tm,),
        in_specs=[row(d)] + [pat] * 6 + [row(d_pool), halo,
                  _const_spec(pool_w.shape), _const_spec((1, d_pool)), _const_spec(w_out.shape)],
        out_specs=row(d),
        out_shape=jax.ShapeDtypeStruct((n, d), jnp.float32),
        scratch_shapes=[pltpu.VMEM((tm + POOL_HALO, d_pool), jnp.float32)],
        compiler_params=_params(1),
        name="mixout_prompt",
    )(x, *flat, u, u, pool_w, pool_scale, w_out)


def _mixout_sample_body(x_ref, attn_ref, u_ref, st_ref, pw_ref, ps_ref, wo_ref, h_ref, pool_ref):
    u = u_ref[...]
    n_st = st_ref.shape[1]
    gdim = u.shape[1] // len(POOL_WINDOWS)
    zs = []
    for gi, w in enumerate(POOL_WINDOWS):
        cols = slice(gi * gdim, (gi + 1) * gdim)
        cur = u[:, cols]
        tot = cur
        for k in range(1, w):
            tot = tot + st_ref[0, n_st - k, :, cols]
        zs.append(tot / float(w) - cur)
    z = jnp.concatenate(zs, axis=1)
    h_ref[...] = _pool_project(x_ref[...], attn_ref[...], z, pw_ref, ps_ref, wo_ref)
    pool_ref[:n_st - 1] = st_ref[0, 1:]
    pool_ref[n_st - 1] = u


def _mixout_sample(x, attn, u, state_t, layer, pool_w, pool_scale, w_out):
    n, d = x.shape
    _, n_st, _, d_pool = state_t.shape
    assert n_st == POOL_WINDOWS[-1] - 1
    full = lambda a: pl.BlockSpec(a.shape, lambda i: (0,) * a.ndim)
    st_spec = pl.BlockSpec((1, n_st, n, d_pool), lambda i: (layer, 0, 0, 0))
    return pl.pallas_call(
        _mixout_sample_body,
        grid=(1,),
        in_specs=[full(x), full(attn), full(u), st_spec, full(pool_w), full(pool_scale), full(w_out)],
        out_specs=[pl.BlockSpec((n, d), lambda i: (0, 0)), pl.BlockSpec((n_st, n, d_pool), lambda i: (0, 0, 0))],
        out_shape=[jax.ShapeDtypeStruct((n, d), jnp.float32),
                   jax.ShapeDtypeStruct((n_st, n, d_pool), jnp.float32)],
        compiler_params=_params(1),
        name="mixout_sample",
    )(x, attn, u, state_t, pool_w, pool_scale, w_out)


def _mlp_ple_body(ff_chunk, with_final, x_ref, p_ref, gm_ref, wu_ref, wd_ref, gp_ref, wg_ref, wp_ref,
                  gf_ref, h_ref, *y_ref):
    x = x_ref[...]
    xn = _rms(x, gm_ref[...]).astype(jnp.bfloat16)
    h = x
    for c in range(wu_ref.shape[1] // ff_chunk):
        a = jnp.maximum(_dot(xn, wu_ref[:, c * ff_chunk:(c + 1) * ff_chunk]), 0.0)
        h = h + _dot((a * a).astype(jnp.bfloat16), wd_ref[c * ff_chunk:(c + 1) * ff_chunk, :])
    gate = jax.nn.sigmoid(_dot(_rms(h, gp_ref[...]).astype(jnp.bfloat16), wg_ref[...]))
    h = h + _dot(p_ref[...].astype(jnp.bfloat16), wp_ref[...]) * gate
    h_ref[...] = h
    if with_final:
        y_ref[0][...] = _rms(h, gf_ref[...])


def _mlp_ple(x, p, layer, g_mlp, w_up, w_down, g_ple, w_gate, w_ple, g_final, tm, with_final):
    n, d = x.shape
    row = lambda c: pl.BlockSpec((tm, c), lambda i: (i, 0))
    p_spec = pl.BlockSpec((tm, p.shape[1]), lambda i: (layer * (n // tm) + i, 0))
    n_out = 2 if with_final else 1
    outs = pl.pallas_call(
        functools.partial(_mlp_ple_body, min(1024, w_up.shape[1]), with_final),
        grid=(n // tm,),
        in_specs=[row(d), p_spec, _const_spec((1, d)), _const_spec(w_up.shape),
                  _const_spec(w_down.shape), _const_spec((1, d)), _const_spec(w_gate.shape),
                  _const_spec(w_ple.shape), _const_spec((1, d))],
        out_specs=[row(d)] * n_out,
        out_shape=[jax.ShapeDtypeStruct((n, d), jnp.float32)] * n_out,
        compiler_params=_params(1),
        name="mlp_ple",
    )(x, p, g_mlp, w_up, w_down, g_ple, w_gate, w_ple, g_final)
    return outs


def kernel(x_prompt, x_sample, cache_attn_kv, state_pool, p_prompt, p_sample, norm_attn_g, w_in, pool_w,
           pool_scale, w_out, norm_mlp_g, w_up, w_down, ple_norm_g, w_ple_gate, w_ple, final_norm_g):
    b, s, d = x_prompt.shape
    nb_s, t_s, _ = x_sample.shape
    depth = w_in.shape[0]
    d_pool = state_pool.shape[-1]
    n_st = state_pool.shape[2]
    w_buf = cache_attn_kv.shape[2]
    kv_keep = min(SPAN * DILATIONS[-1], s)
    assert t_s == 1 and s % (QBLK * DILATIONS[-1]) == 0
    tm = 512
    bf = lambda a: a.astype(jnp.bfloat16)
    row = lambda a: a.reshape(1, -1)

    hp = x_prompt.reshape(b * s, d)
    hs = x_sample.reshape(nb_s, d)
    g_final = row(final_norm_g)
    pp_all = p_prompt.reshape(depth * b * s, -1)
    ps_all = p_sample.reshape(depth * nb_s, -1)
    cache_t = cache_attn_kv.transpose(0, 1, 3, 4, 5, 2).reshape(depth * nb_s, 2, D_ATTN, w_buf)
    state_t = state_pool.transpose(0, 2, 1, 3)
    kv_p, kv_s, pool_p, pool_s = [], [], [], []
    yp = ys = None
    for i in range(depth):
        last = i == depth - 1
        wi, wo, wu, wd = bf(w_in[i]), bf(w_out[i]), bf(w_up[i]), bf(w_down[i])
        wg, wp, pw = bf(w_ple_gate[i]), bf(w_ple[i]), bf(pool_w[i])
        ga, gm, gp, ps = row(norm_attn_g[i]), row(norm_mlp_g[i]), row(ple_norm_g[i]), row(pool_scale[i])

        qs, kvs, kvt, u = _inproj_prompt(hp, ga, wi, tm, s, kv_keep)
        pats = [_attn_prompt(q, kv, dil) for q, kv, dil in zip(qs, kvs, DILATIONS)]
        hp = _mixout_prompt(hp, pats, u, pw, ps, wo, s, tm)
        outs = _mlp_ple(hp, pp_all, i, gm, wu, wd, gp, wg, wp, g_final, tm, last)
        hp = outs[0]
        if last:
            yp = outs[1]
        kv_p.append(kvt)
        pool_p.append(u.reshape(b, s, d_pool)[:, s - n_st:])

        q, kvf, u = _inproj_sample(hs, ga, wi)
        attn = _sample_attn(q.reshape(nb_s, D_ATTN, 1), kvf.reshape(nb_s, 2 * D_ATTN, 1), cache_t, i)
        hs, pool_new = _mixout_sample(hs, attn.reshape(nb_s, D_ATTN), u, state_t, i, pw, ps, wo)
        outs = _mlp_ple(hs, ps_all, i, gm, wu, wd, gp, wg, wp, g_final, nb_s, last)
        hs = outs[0]
        if last:
            ys = outs[1]
        kv_s.append(kvf.reshape(nb_s, 1, 2, N_HEADS, HEAD_DIM))
        pool_s.append(pool_new)

    kv_prompt = jnp.stack(kv_p).reshape(depth, b, 2, N_HEADS, HEAD_DIM, kv_keep).transpose(0, 1, 5, 2, 3, 4)
    pool_sample = jnp.stack(pool_s).transpose(0, 2, 1, 3)
    return (yp.reshape(b, s, d), ys.reshape(nb_s, 1, d), kv_prompt, jnp.stack(kv_s),
            jnp.stack(pool_p), pool_sample)
```

```python
import functools

import jax
import jax.numpy as jnp
from jax import lax
from jax.experimental import pallas as pl
from jax.experimental.pallas import tpu as pltpu

N_HEADS = 8
HEAD_DIM = 64
D_ATTN = N_HEADS * HEAD_DIM
POOL_WINDOWS = (2, 4, 8, 16)
POOL_HALO = 16
DILATIONS = (1, 4, 16)
SPAN = 128
QBLK = 128
ATTN_UNROLL = 8
LANES = 128
SUBLANES = 8
EPS = 1e-6
NEG_INF = -1e30
ATTN_SCALE = HEAD_DIM ** -0.5
LOG2E = 1.4426950408889634
LN2 = 0.6931471805599453
VMEM_LIMIT = 56 * 1024 * 1024


def _rms(x, g):
    return x * lax.rsqrt(jnp.mean(x * x, axis=-1, keepdims=True) + EPS) * g


def _dot(a, b):
    return jnp.dot(a, b, preferred_element_type=jnp.float32)


def _dot_nt(a, b):
    return lax.dot_general(a, b, (((1,), (1,)), ((), ())), preferred_element_type=jnp.float32)


def _layer_spec(stacked, layer):
    nd = stacked.ndim - 1
    return pl.BlockSpec((None,) + stacked.shape[1:], lambda *_: (layer,) + (0,) * nd,
                        pipeline_mode=pl.Buffered(1))


def _params(n_grid):
    return pltpu.CompilerParams(dimension_semantics=("arbitrary",) * n_grid,
                                vmem_limit_bytes=VMEM_LIMIT)


def _project(x_ref, g_ref, w_ref, q_scale):
    xn = _rms(x_ref[...], g_ref[...]).astype(jnp.bfloat16)
    proj = _dot(xn, w_ref[...])
    return proj[:, :D_ATTN] * q_scale, proj[:, D_ATTN:3 * D_ATTN], proj[:, 3 * D_ATTN:]


def _inproj_sample_body(x_ref, g_ref, w_ref, q_ref, kvf_ref, u_ref):
    q, kvf, u = _project(x_ref, g_ref, w_ref, ATTN_SCALE)
    q_ref[...] = q
    kvf_ref[...] = kvf
    u_ref[...] = u


def _inproj_sample(x, g, w, layer):
    n, d = x.shape
    d_in = w.shape[-1]
    d_pool = d_in - 3 * D_ATTN
    full = lambda r, c: pl.BlockSpec((r, c), lambda i: (0, 0))
    return pl.pallas_call(
        _inproj_sample_body,
        grid=(1,),
        in_specs=[full(n, d), _layer_spec(g, layer), _layer_spec(w, layer)],
        out_specs=[full(n, D_ATTN), full(n, 2 * D_ATTN), full(n, d_pool)],
        out_shape=[jax.ShapeDtypeStruct((n, D_ATTN), jnp.float32),
                   jax.ShapeDtypeStruct((n, 2 * D_ATTN), jnp.float32),
                   jax.ShapeDtypeStruct((n, d_pool), jnp.float32)],
        compiler_params=_params(1),
        name="inproj_sample",
    )(x, g, w)


def _inproj_prompt_body(first_kept, has_carry, x_ref, g_ref, w_ref, *refs):
    if has_carry:
        refs = refs[1:]
    n_pat = len(DILATIONS)
    q_refs, kv_refs = refs[:n_pat], refs[n_pat:2 * n_pat]
    kvt_ref, u_ref, buf_a, buf_b = refs[2 * n_pat:]
    tm = x_ref.shape[0]
    q, kvf, u = _project(x_ref, g_ref, w_ref, ATTN_SCALE * LOG2E)
    u_ref[...] = u
    n_q = D_ATTN // LANES
    n_grp = buf_a.shape[0]

    def emit(level, r, c, rows):
        dst, c0 = (q_refs[level], c) if c < n_q else (kv_refs[level], c - n_q)
        dst[0, r, :, c0 * LANES:(c0 + 1) * LANES] = rows.astype(jnp.bfloat16)

    for c in range(n_grp):
        src, c0 = (q, c) if c < n_q else (kvf, c - n_q)
        rows = src[:, c0 * LANES:(c0 + 1) * LANES]
        buf_a[c] = rows
        emit(0, 0, c, rows)
    src_buf, dst_buf = buf_a, buf_b
    for level in range(1, n_pat):
        d_prev, dil = DILATIONS[level - 1], DILATIONS[level]
        step = dil // d_prev
        rows_prev, rows_cur = tm // d_prev, tm // dil
        keep = level + 1 < n_pat
        for r in range(dil):
            r_lo, r_hi = r % d_prev, r // d_prev
            for c in range(n_grp):
                rows = src_buf[c, pl.ds(r_lo * rows_prev + r_hi, rows_cur, stride=step), :]
                if keep:
                    dst_buf[c, r * rows_cur:(r + 1) * rows_cur, :] = rows
                emit(level, r, c, rows)
        src_buf, dst_buf = dst_buf, src_buf

    @pl.when(pl.program_id(1) >= first_kept)
    def _():
        kvt_ref[0] = kvf.T


def _inproj_prompt(x, g, w, layer, kvt_all, tm, seq, keep):
    n, d = x.shape
    depth, _, d_in = w.shape
    d_pool = d_in - 3 * D_ATTN
    b, bps = n // seq, seq // tm
    first_kept = (seq - keep) // tm
    row = lambda c: pl.BlockSpec((tm, c), lambda bi, j: (bi * bps + j, 0))
    cls = lambda dil, c: pl.BlockSpec((1, dil, tm // dil, c), lambda bi, j: (bi, 0, j, 0))
    cls_shape = lambda dil, c: jax.ShapeDtypeStruct((b, dil, seq // dil, c), jnp.bfloat16)
    kvt_spec = pl.BlockSpec((1, 2 * D_ATTN, tm),
                            lambda bi, j: (layer * b + bi, 0, jnp.maximum(j - first_kept, 0)))
    n_pat = len(DILATIONS)
    has_carry = kvt_all is not None
    carry_specs = [pl.BlockSpec(memory_space=pl.ANY)] if has_carry else []
    carry_args = (kvt_all,) if has_carry else ()
    n_grp = 3 * D_ATTN // LANES
    outs = pl.pallas_call(
        functools.partial(_inproj_prompt_body, first_kept, has_carry),
        grid=(b, bps),
        in_specs=[row(d), _layer_spec(g, layer), _layer_spec(w, layer)] + carry_specs,
        out_specs=[cls(dil, D_ATTN) for dil in DILATIONS] + [cls(dil, 2 * D_ATTN) for dil in DILATIONS]
                  + [kvt_spec, row(d_pool)],
        out_shape=[cls_shape(dil, D_ATTN) for dil in DILATIONS]
                  + [cls_shape(dil, 2 * D_ATTN) for dil in DILATIONS]
                  + [jax.ShapeDtypeStruct((depth * b, 2 * D_ATTN, keep), jnp.float32),
                     jax.ShapeDtypeStruct((n, d_pool), jnp.float32)],
        scratch_shapes=[pltpu.VMEM((n_grp, tm, LANES), jnp.float32)] * 2,
        input_output_aliases={3: 2 * n_pat} if has_carry else {},
        compiler_params=_params(2),
        name="inproj_prompt",
    )(x, g, w, *carry_args)
    return outs[:n_pat], outs[n_pat:2 * n_pat], outs[2 * n_pat], outs[2 * n_pat + 1]


def _attn_body(dil, rblocks, q_ref, kc_ref, kp_ref, o_ref, l_ref):
    n = pl.program_id(1)
    qi = lax.broadcasted_iota(jnp.int32, (QBLK, 2 * QBLK), 0)
    kj = lax.broadcasted_iota(jnp.int32, (QBLK, 2 * QBLK), 1) - QBLK
    rel = qi - kj
    band = (rel >= 0) & (rel <= SPAN)
    band_first = band & ((kj >= 0) | (n > 0))
    lane = lax.broadcasted_iota(jnp.int32, (QBLK, LANES), 1)
    low = lane < HEAD_DIM

    def one_block(r, j):
        mask = band_first if j == 0 else band
        q_rows = slice(j * QBLK, (j + 1) * QBLK)
        start = j * QBLK * dil + r
        rows = pl.ds(start, QBLK) if dil == 1 else pl.ds(start, QBLK, stride=dil)
        for p in range(D_ATTN // LANES):
            cols = slice(p * LANES, (p + 1) * LANES)
            vcols = slice(D_ATTN + p * LANES, D_ATTN + (p + 1) * LANES)
            qp = q_ref[0, r, q_rows, cols]
            if j == 0:
                k_prev, v_prev = kp_ref[0, r, :, cols], kp_ref[0, r, :, vcols]
            else:
                p_rows = slice((j - 1) * QBLK, j * QBLK)
                k_prev, v_prev = kc_ref[0, r, p_rows, cols], kc_ref[0, r, p_rows, vcols]
            kk = jnp.concatenate([k_prev, kc_ref[0, r, q_rows, cols]], axis=0)
            vv = jnp.concatenate([v_prev, kc_ref[0, r, q_rows, vcols]], axis=0)
            accs, ms, dens = [], [], []
            for sel in (low, ~low):
                qh = jnp.where(sel, qp, jnp.zeros_like(qp))
                s = _dot_nt(qh, kk)
                s = jnp.where(mask, s, NEG_INF)
                m = jnp.max(s, axis=-1, keepdims=True)
                e = jnp.exp2(s - m)
                ms.append(m)
                dens.append(jnp.sum(e, axis=-1, keepdims=True))
                accs.append(_dot(e.astype(jnp.bfloat16), vv))
            den = jnp.where(low, dens[0], dens[1])
            o_ref[0, p, rows, :] = jnp.where(low, accs[0], accs[1]) / den
            l_ref[0, p, rows, :] = (jnp.where(low, ms[0], ms[1]) + jnp.log2(den)) * LN2

    if dil <= ATTN_UNROLL:
        for r in range(dil):
            for j in range(rblocks):
                one_block(r, j)
    else:
        def classes(i, carry):
            for k in range(ATTN_UNROLL):
                for j in range(rblocks):
                    one_block(i * ATTN_UNROLL + k, j)
            return carry
        lax.fori_loop(0, dil // ATTN_UNROLL, classes, None)


def _attn_prompt(q, kv, dil):
    b, _, sc, _ = q.shape
    rblocks = max(ATTN_UNROLL // dil, 1)
    rows = QBLK * rblocks
    nb = sc // rows
    cur = lambda c: pl.BlockSpec((1, dil, rows, c), lambda bi, n: (bi, 0, n, 0))
    prev = pl.BlockSpec((1, dil, QBLK, 2 * D_ATTN), lambda bi, n: (bi, 0, jnp.maximum(n * rblocks - 1, 0), 0))
    n_pair = D_ATTN // LANES
    out = pl.BlockSpec((1, n_pair, rows * dil, LANES), lambda bi, n: (bi, 0, n, 0))
    return pl.pallas_call(
        functools.partial(_attn_body, dil, rblocks),
        grid=(b, nb),
        in_specs=[cur(D_ATTN), cur(2 * D_ATTN), prev],
        out_specs=[out, out],
        out_shape=[jax.ShapeDtypeStruct((b, n_pair, sc * dil, LANES), jnp.float32)] * 2,
        compiler_params=_params(2),
        name=f"attn_d{dil}",
    )(q, kv, kv)


def _to_column(row):
    return jnp.broadcast_to(row, (LANES, row.shape[1])).T[:, :1]


def _to_row(col):
    return jnp.broadcast_to(col, (col.shape[0], LANES)).T[:1, :]


def _sample_attn_body(q_ref, kvn_ref, c_ref, o_ref):
    w_buf = c_ref.shape[-1]
    pos = lax.broadcasted_iota(jnp.int32, (1, w_buf), 1)
    dist = w_buf - pos
    mult = jnp.zeros((1, w_buf), jnp.float32)
    for dil in DILATIONS:
        mult = mult + ((dist % dil == 0) & (dist <= SPAN * dil)).astype(jnp.float32)
    q_col = _to_column(q_ref[0])
    kvn_col = _to_column(kvn_ref[0])
    rows, news = [], []
    for h in range(N_HEADS):
        acc = None
        for g in range(HEAD_DIM // SUBLANES):
            ch = slice(h * HEAD_DIM + g * SUBLANES, h * HEAD_DIM + (g + 1) * SUBLANES)
            part = c_ref[0, 0, ch, :] * q_col[ch, :]
            acc = part if acc is None else acc + part
        rows.append(jnp.sum(acc, axis=0, keepdims=True))
        hs = slice(h * HEAD_DIM, (h + 1) * HEAD_DIM)
        news.append(jnp.sum(q_col[hs, :] * kvn_col[hs, :], axis=0, keepdims=True))
    s = jnp.where(mult > 0.0, jnp.concatenate(rows, axis=0), NEG_INF)
    s_new = jnp.concatenate(news, axis=0)
    m = jnp.maximum(jnp.max(s, axis=-1, keepdims=True), s_new)
    p = mult * jnp.exp(s - m)
    p_new = float(len(DILATIONS)) * jnp.exp(s_new - m)
    den = jnp.sum(p, axis=-1, keepdims=True) + p_new
    outs = []
    for h in range(N_HEADS):
        hs = slice(h * HEAD_DIM, (h + 1) * HEAD_DIM)
        vs = slice(D_ATTN + h * HEAD_DIM, D_ATTN + (h + 1) * HEAD_DIM)
        pv = jnp.sum(c_ref[0, 1, hs, :] * p[h:h + 1, :], axis=1, keepdims=True)
        outs.append((pv + p_new[h:h + 1, :] * kvn_col[vs, :]) / den[h:h + 1, :])
    o_ref[0] = _to_row(jnp.concatenate(outs, axis=0))


def _sample_attn(q, kvf, cache_t, layer):
    nb = q.shape[0]
    w_buf = cache_t.shape[-1]
    assert w_buf == SPAN * DILATIONS[-1]
    one = lambda c: pl.BlockSpec((1, 1, c), lambda i: (i, 0, 0))
    out = pl.pallas_call(
        _sample_attn_body,
        grid=(nb,),
        in_specs=[one(D_ATTN), one(2 * D_ATTN),
                  pl.BlockSpec((1, 2, D_ATTN, w_buf), lambda i: (layer * nb + i, 0, 0, 0))],
        out_specs=one(D_ATTN),
        out_shape=jax.ShapeDtypeStruct((nb, 1, D_ATTN), jnp.float32),
        compiler_params=_params(1),
        name="sample_attn",
    )(q.reshape(nb, 1, D_ATTN), kvf.reshape(nb, 1, 2 * D_ATTN), cache_t)
    return out.reshape(nb, D_ATTN)


def _pool_project(x, attn, z, pw_ref, ps_ref, wo_ref):
    gdim = pw_ref.shape[1]
    zb = z.astype(jnp.bfloat16)
    pooled = jnp.concatenate(
        [_dot(zb[:, gi * gdim:(gi + 1) * gdim], pw_ref[gi]) for gi in range(pw_ref.shape[0])], axis=1)
    pooled = pooled * ps_ref[...]
    mix = jnp.concatenate([attn.astype(jnp.bfloat16), pooled.astype(jnp.bfloat16)], axis=1)
    return x + _dot(mix, wo_ref[...])


def _mixout_prompt_body(blocks_per_seq, x_ref, o1_ref, l1_ref, o2_ref, l2_ref, o3_ref, l3_ref,
                        u_ref, halo_ref, pw_ref, ps_ref, wo_ref, h_ref, ubuf):
    tm = x_ref.shape[0]
    pairs = []
    for p in range(o1_ref.shape[1]):
        l1, l2, l3 = l1_ref[0, p], l2_ref[0, p], l3_ref[0, p]
        m = jnp.maximum(jnp.maximum(l1, l2), l3)
        e1, e2, e3 = jnp.exp(l1 - m), jnp.exp(l2 - m), jnp.exp(l3 - m)
        pairs.append((e1 * o1_ref[0, p] + e2 * o2_ref[0, p] + e3 * o3_ref[0, p]) / (e1 + e2 + e3))
    attn = jnp.concatenate(pairs, axis=1)

    blk = pl.program_id(0) % blocks_per_seq
    u = u_ref[...]
    ubuf[:POOL_HALO, :] = jnp.where(blk > 0, halo_ref[...], 0.0)
    ubuf[POOL_HALO:, :] = u
    gdim = u.shape[1] // len(POOL_WINDOWS)
    pos = blk * tm + lax.broadcasted_iota(jnp.int32, (tm, gdim), 0)
    zs = []
    for gi, w in enumerate(POOL_WINDOWS):
        cols = slice(gi * gdim, (gi + 1) * gdim)
        cur = u[:, cols]
        tot = cur
        for k in range(1, w):
            tot = tot + ubuf[POOL_HALO - k:POOL_HALO - k + tm, cols]
        cnt = jnp.minimum(pos + 1, w).astype(jnp.float32)
        zs.append(tot / cnt - cur)
    z = jnp.concatenate(zs, axis=1)
    h_ref[...] = _pool_project(x_ref[...], attn, z, pw_ref, ps_ref, wo_ref)


def _mixout_prompt(x, pats, u, pool_w, pool_scale, w_out, layer, seq, tm):
    n, d = x.shape
    d_pool = u.shape[1]
    row = lambda c: pl.BlockSpec((tm, c), lambda i: (i, 0))
    halo = pl.BlockSpec((POOL_HALO, d_pool), lambda i: (jnp.maximum(i * (tm // POOL_HALO) - 1, 0), 0))
    flat = [a for pair in pats for a in pair]
    bps = seq // tm
    pat = pl.BlockSpec((1, D_ATTN // LANES, tm, LANES), lambda i: (i // bps, 0, i % bps, 0))
    return pl.pallas_call(
        functools.partial(_mixout_prompt_body, bps),
        grid=(n // tm,),
        in_specs=[row(d)] + [pat] * 6 + [row(d_pool), halo, _layer_spec(pool_w, layer),
                  _layer_spec(pool_scale, layer), _layer_spec(w_out, layer)],
        out_specs=row(d),
        out_shape=jax.ShapeDtypeStruct((n, d), jnp.float32),
        scratch_shapes=[pltpu.VMEM((tm + POOL_HALO, d_pool), jnp.float32)],
        compiler_params=_params(1),
        name="mixout_prompt",
    )(x, *flat, u, u, pool_w, pool_scale, w_out)


def _mixout_sample_body(x_ref, attn_ref, u_ref, st_ref, pw_ref, ps_ref, wo_ref, h_ref, pool_ref):
    u = u_ref[...]
    n_st = st_ref.shape[0]
    gdim = u.shape[1] // len(POOL_WINDOWS)
    zs = []
    for gi, w in enumerate(POOL_WINDOWS):
        cols = slice(gi * gdim, (gi + 1) * gdim)
        cur = u[:, cols]
        tot = cur
        for k in range(1, w):
            tot = tot + st_ref[n_st - k, :, cols]
        zs.append(tot / float(w) - cur)
    z = jnp.concatenate(zs, axis=1)
    h_ref[...] = _pool_project(x_ref[...], attn_ref[...], z, pw_ref, ps_ref, wo_ref)
    pool_ref[:n_st - 1] = st_ref[1:]
    pool_ref[n_st - 1] = u


def _mixout_sample(x, attn, u, state_t, pool_w, pool_scale, w_out, layer):
    n, d = x.shape
    _, n_st, _, d_pool = state_t.shape
    assert n_st == POOL_WINDOWS[-1] - 1
    full = lambda a: pl.BlockSpec(a.shape, lambda i: (0,) * a.ndim)
    stacked = lambda a: _layer_spec(a, layer)
    return pl.pallas_call(
        _mixout_sample_body,
        grid=(1,),
        in_specs=[full(x), full(attn), full(u), stacked(state_t), stacked(pool_w), stacked(pool_scale),
                  stacked(w_out)],
        out_specs=[pl.BlockSpec((n, d), lambda i: (0, 0)), pl.BlockSpec((n_st, n, d_pool), lambda i: (0, 0, 0))],
        out_shape=[jax.ShapeDtypeStruct((n, d), jnp.float32),
                   jax.ShapeDtypeStruct((n_st, n, d_pool), jnp.float32)],
        compiler_params=_params(1),
        name="mixout_sample",
    )(x, attn, u, state_t, pool_w, pool_scale, w_out)


def _mlp_ple_body(ff_chunk, with_final, x_ref, p_ref, gm_ref, wu_ref, wd_ref, gp_ref, wg_ref, wp_ref,
                  gf_ref, h_ref, *y_ref):
    x = x_ref[...]
    xn = _rms(x, gm_ref[...]).astype(jnp.bfloat16)
    h = x
    for c in range(wu_ref.shape[1] // ff_chunk):
        a = jnp.maximum(_dot(xn, wu_ref[:, c * ff_chunk:(c + 1) * ff_chunk]), 0.0)
        h = h + _dot((a * a).astype(jnp.bfloat16), wd_ref[c * ff_chunk:(c + 1) * ff_chunk, :])
    gate = jax.nn.sigmoid(_dot(_rms(h, gp_ref[...]).astype(jnp.bfloat16), wg_ref[...]))
    h = h + _dot(p_ref[...].astype(jnp.bfloat16), wp_ref[...]) * gate
    h_ref[...] = h
    if with_final:
        y_ref[0][...] = _rms(h, gf_ref[...])


def _mlp_ple(x, p, layer, g_mlp, w_up, w_down, g_ple, w_gate, w_ple, g_final, tm, with_final):
    n, d = x.shape
    row = lambda c: pl.BlockSpec((tm, c), lambda i: (i, 0))
    p_spec = pl.BlockSpec((tm, p.shape[1]), lambda i: (layer * (n // tm) + i, 0))
    stacked = lambda a: _layer_spec(a, layer)
    n_out = 2 if with_final else 1
    outs = pl.pallas_call(
        functools.partial(_mlp_ple_body, min(1024, w_up.shape[-1]), with_final),
        grid=(n // tm,),
        in_specs=[row(d), p_spec, stacked(g_mlp), stacked(w_up), stacked(w_down), stacked(g_ple),
                  stacked(w_gate), stacked(w_ple), pl.BlockSpec((1, d), lambda i: (0, 0))],
        out_specs=[row(d)] * n_out,
        out_shape=[jax.ShapeDtypeStruct((n, d), jnp.float32)] * n_out,
        compiler_params=_params(1),
        name="mlp_ple",
    )(x, p, g_mlp, w_up, w_down, g_ple, w_gate, w_ple, g_final)
    return outs


def kernel(x_prompt, x_sample, cache_attn_kv, state_pool, p_prompt, p_sample, norm_attn_g, w_in, pool_w,
           pool_scale, w_out, norm_mlp_g, w_up, w_down, ple_norm_g, w_ple_gate, w_ple, final_norm_g):
    b, s, d = x_prompt.shape
    nb_s, t_s, _ = x_sample.shape
    depth = w_in.shape[0]
    d_pool = state_pool.shape[-1]
    n_st = state_pool.shape[2]
    w_buf = cache_attn_kv.shape[2]
    kv_keep = min(SPAN * DILATIONS[-1], s)
    assert t_s == 1 and s % (QBLK * DILATIONS[-1]) == 0
    tm = 512
    bf = lambda a: a.astype(jnp.bfloat16)
    rows = lambda a: a.reshape(depth, 1, -1)

    wi, wo, wu, wd = bf(w_in), bf(w_out), bf(w_up), bf(w_down)
    wg, wp, pw = bf(w_ple_gate), bf(w_ple), bf(pool_w)
    ga, gm, gp, ps = rows(norm_attn_g), rows(norm_mlp_g), rows(ple_norm_g), rows(pool_scale)
    g_final = final_norm_g.reshape(1, -1)

    hp = x_prompt.reshape(b * s, d)
    hs = x_sample.reshape(nb_s, d)
    pp_all = p_prompt.reshape(depth * b * s, -1)
    ps_all = p_sample.reshape(depth * nb_s, -1)
    cache_t = cache_attn_kv.transpose(0, 1, 3, 4, 5, 2).reshape(depth * nb_s, 2, D_ATTN, w_buf)
    state_t = state_pool.transpose(0, 2, 1, 3)
    kvt_all = None
    kv_s, pool_p, pool_s = [], [], []
    yp = ys = None
    for i in range(depth):
        last = i == depth - 1

        qs, kvs, kvt_all, u = _inproj_prompt(hp, ga, wi, i, kvt_all, tm, s, kv_keep)
        pats = [_attn_prompt(q, kv, dil) for q, kv, dil in zip(qs, kvs, DILATIONS)]
        hp = _mixout_prompt(hp, pats, u, pw, ps, wo, i, s, tm)
        outs = _mlp_ple(hp, pp_all, i, gm, wu, wd, gp, wg, wp, g_final, tm, last)
        hp = outs[0]
        if last:
            yp = outs[1]
        pool_p.append(u.reshape(b, s, d_pool)[:, s - n_st:])

        q, kvf, u = _inproj_sample(hs, ga, wi, i)
        attn = _sample_attn(q, kvf, cache_t, i)
        hs, pool_new = _mixout_sample(hs, attn, u, state_t, pw, ps, wo, i)
        outs = _mlp_ple(hs, ps_all, i, gm, wu, wd, gp, wg, wp, g_final, nb_s, last)
        hs = outs[0]
        if last:
            ys = outs[1]
        kv_s.append(kvf.reshape(nb_s, 1, 2, N_HEADS, HEAD_DIM))
        pool_s.append(pool_new)

    kv_prompt = kvt_all.reshape(depth, b, 2, N_HEADS, HEAD_DIM, kv_keep).transpose(0, 1, 5, 2, 3, 4)
    pool_sample = jnp.stack(pool_s).transpose(0, 2, 1, 3)
    return (yp.reshape(b, s, d), ys.reshape(nb_s, 1, d), kv_prompt, jnp.stack(kv_s),
            jnp.stack(pool_p), pool_sample)
```

```python
import functools

import jax
import jax.numpy as jnp
from jax import lax
from jax.experimental import pallas as pl
from jax.experimental.pallas import tpu as pltpu

N_HEADS = 8
HEAD_DIM = 64
D_ATTN = N_HEADS * HEAD_DIM
POOL_WINDOWS = (2, 4, 8, 16)
POOL_HALO = 16
DILATIONS = (1, 4, 16)
SPAN = 128
QBLK = 128
ATTN_UNROLL = 8
FF_CHUNK = 1024
LANES = 128
SUBLANES = 8
EPS = 1e-6
NEG_INF = -1e30
ATTN_SCALE = HEAD_DIM ** -0.5
LOG2E = 1.4426950408889634
LN2 = 0.6931471805599453
VMEM_LIMIT = 56 * 1024 * 1024


def _rms(x, g):
    return x * lax.rsqrt(jnp.mean(x * x, axis=-1, keepdims=True) + EPS) * g


def _dot(a, b):
    return jnp.dot(a, b, preferred_element_type=jnp.float32)


def _dot_nt(a, b):
    return lax.dot_general(a, b, (((1,), (1,)), ((), ())), preferred_element_type=jnp.float32)


def _layer_spec(stacked, layer):
    nd = stacked.ndim - 1
    return pl.BlockSpec((None,) + stacked.shape[1:], lambda *_: (layer,) + (0,) * nd,
                        pipeline_mode=pl.Buffered(1))


def _params(n_grid):
    return pltpu.CompilerParams(dimension_semantics=("arbitrary",) * n_grid,
                                vmem_limit_bytes=VMEM_LIMIT)


def _project(x_ref, g_ref, w_ref, q_scale):
    xn = _rms(x_ref[...], g_ref[...]).astype(jnp.bfloat16)
    proj = _dot(xn, w_ref[...])
    return proj[:, :D_ATTN] * q_scale, proj[:, D_ATTN:3 * D_ATTN], proj[:, 3 * D_ATTN:]


def _inproj_sample_body(x_ref, g_ref, w_ref, q_ref, kvf_ref, u_ref):
    q, kvf, u = _project(x_ref, g_ref, w_ref, ATTN_SCALE)
    q_ref[...] = q
    kvf_ref[...] = kvf
    u_ref[...] = u


def _inproj_sample(x, g, w, layer):
    n, d = x.shape
    d_in = w.shape[-1]
    d_pool = d_in - 3 * D_ATTN
    full = lambda r, c: pl.BlockSpec((r, c), lambda i: (0, 0))
    return pl.pallas_call(
        _inproj_sample_body,
        grid=(1,),
        in_specs=[full(n, d), _layer_spec(g, layer), _layer_spec(w, layer)],
        out_specs=[full(n, D_ATTN), full(n, 2 * D_ATTN), full(n, d_pool)],
        out_shape=[jax.ShapeDtypeStruct((n, D_ATTN), jnp.float32),
                   jax.ShapeDtypeStruct((n, 2 * D_ATTN), jnp.float32),
                   jax.ShapeDtypeStruct((n, d_pool), jnp.float32)],
        compiler_params=_params(1),
        name="inproj_sample",
    )(x, g, w)


def _inproj_prompt_body(first_kept, has_carry, x_ref, g_ref, w_ref, *refs):
    if has_carry:
        refs = refs[1:]
    n_pat = len(DILATIONS)
    q_refs, kv_refs = refs[:n_pat], refs[n_pat:2 * n_pat]
    kvt_ref, u_ref, buf_a, buf_b = refs[2 * n_pat:]
    tm = x_ref.shape[0]
    q, kvf, u = _project(x_ref, g_ref, w_ref, ATTN_SCALE * LOG2E)
    u_ref[...] = u
    n_q = D_ATTN // LANES
    n_grp = buf_a.shape[0]

    def emit(level, r, c, rows):
        dst, c0 = (q_refs[level], c) if c < n_q else (kv_refs[level], c - n_q)
        dst[0, r, :, c0 * LANES:(c0 + 1) * LANES] = rows.astype(jnp.bfloat16)

    for c in range(n_grp):
        src, c0 = (q, c) if c < n_q else (kvf, c - n_q)
        rows = src[:, c0 * LANES:(c0 + 1) * LANES]
        buf_a[c] = rows
        emit(0, 0, c, rows)
    src_buf, dst_buf = buf_a, buf_b
    for level in range(1, n_pat):
        d_prev, dil = DILATIONS[level - 1], DILATIONS[level]
        step = dil // d_prev
        rows_prev, rows_cur = tm // d_prev, tm // dil
        keep = level + 1 < n_pat
        for r in range(dil):
            r_lo, r_hi = r % d_prev, r // d_prev
            for c in range(n_grp):
                rows = src_buf[c, pl.ds(r_lo * rows_prev + r_hi, rows_cur, stride=step), :]
                if keep:
                    dst_buf[c, r * rows_cur:(r + 1) * rows_cur, :] = rows
                emit(level, r, c, rows)
        src_buf, dst_buf = dst_buf, src_buf

    @pl.when(pl.program_id(1) >= first_kept)
    def _():
        kvt_ref[0] = kvf.T


def _inproj_prompt(x, g, w, layer, kvt_all, tm, seq, keep):
    n, d = x.shape
    depth, _, d_in = w.shape
    d_pool = d_in - 3 * D_ATTN
    b, bps = n // seq, seq // tm
    first_kept = (seq - keep) // tm
    row = lambda c: pl.BlockSpec((tm, c), lambda bi, j: (bi * bps + j, 0))
    cls = lambda dil, c: pl.BlockSpec((1, dil, tm // dil, c), lambda bi, j: (bi, 0, j, 0))
    cls_shape = lambda dil, c: jax.ShapeDtypeStruct((b, dil, seq // dil, c), jnp.bfloat16)
    kvt_spec = pl.BlockSpec((1, 2 * D_ATTN, tm),
                            lambda bi, j: (layer * b + bi, 0, jnp.maximum(j - first_kept, 0)))
    n_pat = len(DILATIONS)
    has_carry = kvt_all is not None
    carry_specs = [pl.BlockSpec(memory_space=pl.ANY)] if has_carry else []
    carry_args = (kvt_all,) if has_carry else ()
    n_grp = 3 * D_ATTN // LANES
    outs = pl.pallas_call(
        functools.partial(_inproj_prompt_body, first_kept, has_carry),
        grid=(b, bps),
        in_specs=[row(d), _layer_spec(g, layer), _layer_spec(w, layer)] + carry_specs,
        out_specs=[cls(dil, D_ATTN) for dil in DILATIONS] + [cls(dil, 2 * D_ATTN) for dil in DILATIONS]
                  + [kvt_spec, row(d_pool)],
        out_shape=[cls_shape(dil, D_ATTN) for dil in DILATIONS]
                  + [cls_shape(dil, 2 * D_ATTN) for dil in DILATIONS]
                  + [jax.ShapeDtypeStruct((depth * b, 2 * D_ATTN, keep), jnp.float32),
                     jax.ShapeDtypeStruct((n, d_pool), jnp.float32)],
        scratch_shapes=[pltpu.VMEM((n_grp, tm, LANES), jnp.float32)] * 2,
        input_output_aliases={3: 2 * n_pat} if has_carry else {},
        compiler_params=_params(2),
        name="inproj_prompt",
    )(x, g, w, *carry_args)
    return outs[:n_pat], outs[n_pat:2 * n_pat], outs[2 * n_pat], outs[2 * n_pat + 1]


def _attn_body(dil, rblocks, q_ref, kc_ref, kp_ref, o_ref, l_ref):
    n = pl.program_id(1)
    qi = lax.broadcasted_iota(jnp.int32, (QBLK, 2 * QBLK), 0)
    kj = lax.broadcasted_iota(jnp.int32, (QBLK, 2 * QBLK), 1) - QBLK
    rel = qi - kj
    band = (rel >= 0) & (rel <= SPAN)
    band_first = band & ((kj >= 0) | (n > 0))
    lane = lax.broadcasted_iota(jnp.int32, (QBLK, LANES), 1)
    low = lane < HEAD_DIM

    def one_block(r, j):
        mask = band_first if j == 0 else band
        q_rows = slice(j * QBLK, (j + 1) * QBLK)
        start = j * QBLK * dil + r
        rows = pl.ds(start, QBLK) if dil == 1 else pl.ds(start, QBLK, stride=dil)
        for p in range(D_ATTN // LANES):
            cols = slice(p * LANES, (p + 1) * LANES)
            vcols = slice(D_ATTN + p * LANES, D_ATTN + (p + 1) * LANES)
            qp = q_ref[0, r, q_rows, cols]
            if j == 0:
                k_prev, v_prev = kp_ref[0, r, :, cols], kp_ref[0, r, :, vcols]
            else:
                p_rows = slice((j - 1) * QBLK, j * QBLK)
                k_prev, v_prev = kc_ref[0, r, p_rows, cols], kc_ref[0, r, p_rows, vcols]
            kk = jnp.concatenate([k_prev, kc_ref[0, r, q_rows, cols]], axis=0)
            vv = jnp.concatenate([v_prev, kc_ref[0, r, q_rows, vcols]], axis=0)
            accs, ms, dens = [], [], []
            for sel in (low, ~low):
                qh = jnp.where(sel, qp, jnp.zeros_like(qp))
                s = _dot_nt(qh, kk)
                s = jnp.where(mask, s, NEG_INF)
                m = jnp.max(s, axis=-1, keepdims=True)
                e = jnp.exp2(s - m)
                ms.append(m)
                dens.append(jnp.sum(e, axis=-1, keepdims=True))
                accs.append(_dot(e.astype(jnp.bfloat16), vv))
            den = jnp.where(low, dens[0], dens[1])
            o_ref[0, p, rows, :] = jnp.where(low, accs[0], accs[1]) / den
            l_ref[0, p, rows, :] = (jnp.where(low, ms[0], ms[1]) + jnp.log2(den)) * LN2

    if dil <= ATTN_UNROLL:
        for r in range(dil):
            for j in range(rblocks):
                one_block(r, j)
    else:
        def classes(i, carry):
            for k in range(ATTN_UNROLL):
                for j in range(rblocks):
                    one_block(i * ATTN_UNROLL + k, j)
            return carry
        lax.fori_loop(0, dil // ATTN_UNROLL, classes, None)


def _attn_prompt(q, kv, dil):
    b, _, sc, _ = q.shape
    rblocks = max(ATTN_UNROLL // dil, 1)
    rows = QBLK * rblocks
    nb = sc // rows
    cur = lambda c: pl.BlockSpec((1, dil, rows, c), lambda bi, n: (bi, 0, n, 0))
    prev = pl.BlockSpec((1, dil, QBLK, 2 * D_ATTN), lambda bi, n: (bi, 0, jnp.maximum(n * rblocks - 1, 0), 0))
    n_pair = D_ATTN // LANES
    out = pl.BlockSpec((1, n_pair, rows * dil, LANES), lambda bi, n: (bi, 0, n, 0))
    return pl.pallas_call(
        functools.partial(_attn_body, dil, rblocks),
        grid=(b, nb),
        in_specs=[cur(D_ATTN), cur(2 * D_ATTN), prev],
        out_specs=[out, out],
        out_shape=[jax.ShapeDtypeStruct((b, n_pair, sc * dil, LANES), jnp.float32)] * 2,
        compiler_params=_params(2),
        name=f"attn_d{dil}",
    )(q, kv, kv)


def _to_column(row):
    return jnp.broadcast_to(row, (LANES, row.shape[1])).T[:, :1]


def _to_row(col):
    return jnp.broadcast_to(col, (col.shape[0], LANES)).T[:1, :]


def _sample_attn_body(q_ref, kvn_ref, c_ref, o_ref):
    w_buf = c_ref.shape[-1]
    pos = lax.broadcasted_iota(jnp.int32, (1, w_buf), 1)
    dist = w_buf - pos
    mult = jnp.zeros((1, w_buf), jnp.float32)
    for dil in DILATIONS:
        mult = mult + ((dist % dil == 0) & (dist <= SPAN * dil)).astype(jnp.float32)
    q_col = _to_column(q_ref[0])
    kvn_col = _to_column(kvn_ref[0])
    rows, news = [], []
    for h in range(N_HEADS):
        acc = None
        for g in range(HEAD_DIM // SUBLANES):
            ch = slice(h * HEAD_DIM + g * SUBLANES, h * HEAD_DIM + (g + 1) * SUBLANES)
            part = c_ref[0, 0, ch, :] * q_col[ch, :]
            acc = part if acc is None else acc + part
        rows.append(jnp.sum(acc, axis=0, keepdims=True))
        hs = slice(h * HEAD_DIM, (h + 1) * HEAD_DIM)
        news.append(jnp.sum(q_col[hs, :] * kvn_col[hs, :], axis=0, keepdims=True))
    s = jnp.where(mult > 0.0, jnp.concatenate(rows, axis=0), NEG_INF)
    s_new = jnp.concatenate(news, axis=0)
    m = jnp.maximum(jnp.max(s, axis=-1, keepdims=True), s_new)
    p = mult * jnp.exp(s - m)
    p_new = float(len(DILATIONS)) * jnp.exp(s_new - m)
    den = jnp.sum(p, axis=-1, keepdims=True) + p_new
    outs = []
    for h in range(N_HEADS):
        hs = slice(h * HEAD_DIM, (h + 1) * HEAD_DIM)
        vs = slice(D_ATTN + h * HEAD_DIM, D_ATTN + (h + 1) * HEAD_DIM)
        pv = jnp.sum(c_ref[0, 1, hs, :] * p[h:h + 1, :], axis=1, keepdims=True)
        outs.append((pv + p_new[h:h + 1, :] * kvn_col[vs, :]) / den[h:h + 1, :])
    o_ref[0] = _to_row(jnp.concatenate(outs, axis=0))


def _sample_attn(q, kvf, cache_t, layer):
    nb = q.shape[0]
    w_buf = cache_t.shape[-1]
    assert w_buf == SPAN * DILATIONS[-1]
    one = lambda c: pl.BlockSpec((1, 1, c), lambda i: (i, 0, 0))
    out = pl.pallas_call(
        _sample_attn_body,
        grid=(nb,),
        in_specs=[one(D_ATTN), one(2 * D_ATTN),
                  pl.BlockSpec((1, 2, D_ATTN, w_buf), lambda i: (layer * nb + i, 0, 0, 0))],
        out_specs=one(D_ATTN),
        out_shape=jax.ShapeDtypeStruct((nb, 1, D_ATTN), jnp.float32),
        compiler_params=_params(1),
        name="sample_attn",
    )(q.reshape(nb, 1, D_ATTN), kvf.reshape(nb, 1, 2 * D_ATTN), cache_t)
    return out.reshape(nb, D_ATTN)


def _pool_project(x, attn, z, pw_ref, ps_ref, wo_ref):
    gdim = pw_ref.shape[1]
    zb = z.astype(jnp.bfloat16)
    pooled = jnp.concatenate(
        [_dot(zb[:, gi * gdim:(gi + 1) * gdim], pw_ref[gi]) for gi in range(pw_ref.shape[0])], axis=1)
    pooled = pooled * ps_ref[...]
    mix = jnp.concatenate([attn.astype(jnp.bfloat16), pooled.astype(jnp.bfloat16)], axis=1)
    return x + _dot(mix, wo_ref[...])


def _mlp_ple_math(x, p_ref, gm_ref, wu_ref, wd_ref, gp_ref, wg_ref, wp_ref):
    xn = _rms(x, gm_ref[...]).astype(jnp.bfloat16)
    h = x
    for c in range(wu_ref.shape[1] // FF_CHUNK):
        a = jnp.maximum(_dot(xn, wu_ref[:, c * FF_CHUNK:(c + 1) * FF_CHUNK]), 0.0)
        h = h + _dot((a * a).astype(jnp.bfloat16), wd_ref[c * FF_CHUNK:(c + 1) * FF_CHUNK, :])
    gate = jax.nn.sigmoid(_dot(_rms(h, gp_ref[...]).astype(jnp.bfloat16), wg_ref[...]))
    return h + _dot(p_ref[...].astype(jnp.bfloat16), wp_ref[...]) * gate


def _post_prompt_body(blocks_per_seq, with_final, x_ref, o1_ref, l1_ref, o2_ref, l2_ref, o3_ref, l3_ref,
                      u_ref, halo_ref, p_ref, pw_ref, ps_ref, wo_ref, gm_ref, wu_ref, wd_ref, gp_ref,
                      wg_ref, wp_ref, gf_ref, h_ref, *rest):
    ubuf = rest[-1]
    tm = x_ref.shape[0]
    pairs = []
    for p in range(o1_ref.shape[1]):
        l1, l2, l3 = l1_ref[0, p], l2_ref[0, p], l3_ref[0, p]
        m = jnp.maximum(jnp.maximum(l1, l2), l3)
        e1, e2, e3 = jnp.exp(l1 - m), jnp.exp(l2 - m), jnp.exp(l3 - m)
        pairs.append((e1 * o1_ref[0, p] + e2 * o2_ref[0, p] + e3 * o3_ref[0, p]) / (e1 + e2 + e3))
    attn = jnp.concatenate(pairs, axis=1)

    blk = pl.program_id(0) % blocks_per_seq
    u = u_ref[...]
    ubuf[:POOL_HALO, :] = jnp.where(blk > 0, halo_ref[...], 0.0)
    ubuf[POOL_HALO:, :] = u
    gdim = u.shape[1] // len(POOL_WINDOWS)
    pos = blk * tm + lax.broadcasted_iota(jnp.int32, (tm, gdim), 0)
    zs = []
    for gi, w in enumerate(POOL_WINDOWS):
        cols = slice(gi * gdim, (gi + 1) * gdim)
        cur = u[:, cols]
        tot = cur
        for k in range(1, w):
            tot = tot + ubuf[POOL_HALO - k:POOL_HALO - k + tm, cols]
        cnt = jnp.minimum(pos + 1, w).astype(jnp.float32)
        zs.append(tot / cnt - cur)
    z = jnp.concatenate(zs, axis=1)
    h = _pool_project(x_ref[...], attn, z, pw_ref, ps_ref, wo_ref)
    h = _mlp_ple_math(h, p_ref, gm_ref, wu_ref, wd_ref, gp_ref, wg_ref, wp_ref)
    h_ref[...] = h
    if with_final:
        rest[0][...] = _rms(h, gf_ref[...])


def _post_prompt(x, pats, u, p, layer, pool_w, pool_scale, w_out, g_mlp, w_up, w_down, g_ple, w_gate, w_ple,
                 g_final, seq, tm, with_final):
    n, d = x.shape
    d_pool = u.shape[1]
    row = lambda c: pl.BlockSpec((tm, c), lambda i: (i, 0))
    halo = pl.BlockSpec((POOL_HALO, d_pool), lambda i: (jnp.maximum(i * (tm // POOL_HALO) - 1, 0), 0))
    flat = [a for pair in pats for a in pair]
    bps = seq // tm
    pat = pl.BlockSpec((1, D_ATTN // LANES, tm, LANES), lambda i: (i // bps, 0, i % bps, 0))
    p_spec = pl.BlockSpec((tm, p.shape[1]), lambda i: (layer * (n // tm) + i, 0))
    stacked = lambda a: _layer_spec(a, layer)
    params = (pool_w, pool_scale, w_out, g_mlp, w_up, w_down, g_ple, w_gate, w_ple)
    n_out = 2 if with_final else 1
    return pl.pallas_call(
        functools.partial(_post_prompt_body, bps, with_final),
        grid=(n // tm,),
        in_specs=[row(d)] + [pat] * 6 + [row(d_pool), halo, p_spec] + [stacked(a) for a in params]
                 + [pl.BlockSpec((1, d), lambda i: (0, 0))],
        out_specs=[row(d)] * n_out,
        out_shape=[jax.ShapeDtypeStruct((n, d), jnp.float32)] * n_out,
        scratch_shapes=[pltpu.VMEM((tm + POOL_HALO, d_pool), jnp.float32)],
        compiler_params=_params(1),
        name="post_prompt",
    )(x, *flat, u, u, p, *params, g_final)


def _mixout_sample_body(x_ref, attn_ref, u_ref, st_ref, pw_ref, ps_ref, wo_ref, h_ref, pool_ref):
    u = u_ref[...]
    n_st = st_ref.shape[0]
    gdim = u.shape[1] // len(POOL_WINDOWS)
    zs = []
    for gi, w in enumerate(POOL_WINDOWS):
        cols = slice(gi * gdim, (gi + 1) * gdim)
        cur = u[:, cols]
        tot = cur
        for k in range(1, w):
            tot = tot + st_ref[n_st - k, :, cols]
        zs.append(tot / float(w) - cur)
    z = jnp.concatenate(zs, axis=1)
    h_ref[...] = _pool_project(x_ref[...], attn_ref[...], z, pw_ref, ps_ref, wo_ref)
    pool_ref[:n_st - 1] = st_ref[1:]
    pool_ref[n_st - 1] = u


def _mixout_sample(x, attn, u, state_t, pool_w, pool_scale, w_out, layer):
    n, d = x.shape
    _, n_st, _, d_pool = state_t.shape
    assert n_st == POOL_WINDOWS[-1] - 1
    full = lambda a: pl.BlockSpec(a.shape, lambda i: (0,) * a.ndim)
    stacked = lambda a: _layer_spec(a, layer)
    return pl.pallas_call(
        _mixout_sample_body,
        grid=(1,),
        in_specs=[full(x), full(attn), full(u), stacked(state_t), stacked(pool_w), stacked(pool_scale),
                  stacked(w_out)],
        out_specs=[pl.BlockSpec((n, d), lambda i: (0, 0)), pl.BlockSpec((n_st, n, d_pool), lambda i: (0, 0, 0))],
        out_shape=[jax.ShapeDtypeStruct((n, d), jnp.float32),
                   jax.ShapeDtypeStruct((n_st, n, d_pool), jnp.float32)],
        compiler_params=_params(1),
        name="mixout_sample",
    )(x, attn, u, state_t, pool_w, pool_scale, w_out)


def _mlp_ple_body(with_final, x_ref, p_ref, gm_ref, wu_ref, wd_ref, gp_ref, wg_ref, wp_ref,
                  gf_ref, h_ref, *y_ref):
    h = _mlp_ple_math(x_ref[...], p_ref, gm_ref, wu_ref, wd_ref, gp_ref, wg_ref, wp_ref)
    h_ref[...] = h
    if with_final:
        y_ref[0][...] = _rms(h, gf_ref[...])


def _mlp_ple(x, p, layer, g_mlp, w_up, w_down, g_ple, w_gate, w_ple, g_final, tm, with_final):
    n, d = x.shape
    row = lambda c: pl.BlockSpec((tm, c), lambda i: (i, 0))
    p_spec = pl.BlockSpec((tm, p.shape[1]), lambda i: (layer * (n // tm) + i, 0))
    stacked = lambda a: _layer_spec(a, layer)
    n_out = 2 if with_final else 1
    outs = pl.pallas_call(
        functools.partial(_mlp_ple_body, with_final),
        grid=(n // tm,),
        in_specs=[row(d), p_spec, stacked(g_mlp), stacked(w_up), stacked(w_down), stacked(g_ple),
                  stacked(w_gate), stacked(w_ple), pl.BlockSpec((1, d), lambda i: (0, 0))],
        out_specs=[row(d)] * n_out,
        out_shape=[jax.ShapeDtypeStruct((n, d), jnp.float32)] * n_out,
        compiler_params=_params(1),
        name="mlp_ple",
    )(x, p, g_mlp, w_up, w_down, g_ple, w_gate, w_ple, g_final)
    return outs


def kernel(x_prompt, x_sample, cache_attn_kv, state_pool, p_prompt, p_sample, norm_attn_g, w_in, pool_w,
           pool_scale, w_out, norm_mlp_g, w_up, w_down, ple_norm_g, w_ple_gate, w_ple, final_norm_g):
    b, s, d = x_prompt.shape
    nb_s, t_s, _ = x_sample.shape
    depth = w_in.shape[0]
    d_pool = state_pool.shape[-1]
    n_st = state_pool.shape[2]
    w_buf = cache_attn_kv.shape[2]
    kv_keep = min(SPAN * DILATIONS[-1], s)
    assert t_s == 1 and s % (QBLK * DILATIONS[-1]) == 0
    tm = 512
    bf = lambda a: a.astype(jnp.bfloat16)
    rows = lambda a: a.reshape(depth, 1, -1)

    wi, wo, wu, wd = bf(w_in), bf(w_out), bf(w_up), bf(w_down)
    wg, wp, pw = bf(w_ple_gate), bf(w_ple), bf(pool_w)
    ga, gm, gp, ps = rows(norm_attn_g), rows(norm_mlp_g), rows(ple_norm_g), rows(pool_scale)
    g_final = final_norm_g.reshape(1, -1)

    hp = x_prompt.reshape(b * s, d)
    hs = x_sample.reshape(nb_s, d)
    pp_all = p_prompt.reshape(depth * b * s, -1)
    ps_all = p_sample.reshape(depth * nb_s, -1)
    cache_t = cache_attn_kv.transpose(0, 1, 3, 4, 5, 2).reshape(depth * nb_s, 2, D_ATTN, w_buf)
    state_t = state_pool.transpose(0, 2, 1, 3)
    kvt_all = None
    kv_s, pool_p, pool_s = [], [], []
    yp = ys = None
    for i in range(depth):
        last = i == depth - 1

        qs, kvs, kvt_all, u = _inproj_prompt(hp, ga, wi, i, kvt_all, tm, s, kv_keep)
        pats = [_attn_prompt(q, kv, dil) for q, kv, dil in zip(qs, kvs, DILATIONS)]
        outs = _post_prompt(hp, pats, u, pp_all, i, pw, ps, wo, gm, wu, wd, gp, wg, wp, g_final, s, tm, last)
        hp = outs[0]
        if last:
            yp = outs[1]
        pool_p.append(u.reshape(b, s, d_pool)[:, s - n_st:])

        q, kvf, u = _inproj_sample(hs, ga, wi, i)
        attn = _sample_attn(q, kvf, cache_t, i)
        hs, pool_new = _mixout_sample(hs, attn, u, state_t, pw, ps, wo, i)
        outs = _mlp_ple(hs, ps_all, i, gm, wu, wd, gp, wg, wp, g_final, nb_s, last)
        hs = outs[0]
        if last:
            ys = outs[1]
        kv_s.append(kvf.reshape(nb_s, 1, 2, N_HEADS, HEAD_DIM))
        pool_s.append(pool_new)

    kv_prompt = kvt_all.reshape(depth, b, 2, N_HEADS, HEAD_DIM, kv_keep).transpose(0, 1, 5, 2, 3, 4)
    pool_sample = jnp.stack(pool_s).transpose(0, 2, 1, 3)
    return (yp.reshape(b, s, d), ys.reshape(nb_s, 1, d), kv_prompt, jnp.stack(kv_s),
            jnp.stack(pool_p), pool_sample)
```

```python
import functools

import jax
import jax.numpy as jnp
from jax import lax
from jax.experimental import pallas as pl
from jax.experimental.pallas import tpu as pltpu

N_HEADS = 8
HEAD_DIM = 64
D_ATTN = N_HEADS * HEAD_DIM
POOL_WINDOWS = (2, 4, 8, 16)
POOL_HALO = 16
DILATIONS = (1, 4, 16)
SPAN = 128
QBLK = 128
ATTN_UNROLL = 8
FF_CHUNK = 1024
LANES = 128
SUBLANES = 8
EPS = 1e-6
NEG_INF = -1e30
ATTN_SCALE = HEAD_DIM ** -0.5
LOG2E = 1.4426950408889634
LN2 = 0.6931471805599453
VMEM_LIMIT = 56 * 1024 * 1024


def _rms(x, g):
    return x * lax.rsqrt(jnp.mean(x * x, axis=-1, keepdims=True) + EPS) * g


def _dot(a, b):
    return jnp.dot(a, b, preferred_element_type=jnp.float32)


def _dot_nt(a, b):
    return lax.dot_general(a, b, (((1,), (1,)), ((), ())), preferred_element_type=jnp.float32)


def _layer_spec(stacked, layer):
    nd = stacked.ndim - 1
    return pl.BlockSpec((None,) + stacked.shape[1:], lambda *_: (layer,) + (0,) * nd,
                        pipeline_mode=pl.Buffered(1))


def _params(n_grid):
    return pltpu.CompilerParams(dimension_semantics=("arbitrary",) * n_grid,
                                vmem_limit_bytes=VMEM_LIMIT)


def _project(x_ref, g_ref, w_ref, q_scale):
    xn = _rms(x_ref[...], g_ref[...]).astype(jnp.bfloat16)
    proj = _dot(xn, w_ref[...])
    return proj[:, :D_ATTN] * q_scale, proj[:, D_ATTN:3 * D_ATTN], proj[:, 3 * D_ATTN:]


def _inproj_sample_body(x_ref, g_ref, w_ref, q_ref, kvf_ref, u_ref):
    q, kvf, u = _project(x_ref, g_ref, w_ref, ATTN_SCALE)
    q_ref[...] = q
    kvf_ref[...] = kvf
    u_ref[...] = u


def _inproj_sample(x, g, w, layer):
    n, d = x.shape
    d_in = w.shape[-1]
    d_pool = d_in - 3 * D_ATTN
    full = lambda r, c: pl.BlockSpec((r, c), lambda i: (0, 0))
    return pl.pallas_call(
        _inproj_sample_body,
        grid=(1,),
        in_specs=[full(n, d), _layer_spec(g, layer), _layer_spec(w, layer)],
        out_specs=[full(n, D_ATTN), full(n, 2 * D_ATTN), full(n, d_pool)],
        out_shape=[jax.ShapeDtypeStruct((n, D_ATTN), jnp.float32),
                   jax.ShapeDtypeStruct((n, 2 * D_ATTN), jnp.float32),
                   jax.ShapeDtypeStruct((n, d_pool), jnp.float32)],
        compiler_params=_params(1),
        name="inproj_sample",
    )(x, g, w)


def _inproj_prompt_body(first_kept, x_ref, g_ref, w_ref, sq_ref, skv_ref, cache_ref, kvt_in_ref, *refs):
    del kvt_in_ref
    n_pat = len(DILATIONS)
    q_refs, kv_refs = refs[:n_pat], refs[n_pat:2 * n_pat]
    kvt_ref, u_ref, sattn_ref, buf_a, buf_b = refs[2 * n_pat:]
    tm = x_ref.shape[0]
    _sample_attn_step(sq_ref, skv_ref, cache_ref, sattn_ref)
    q, kvf, u = _project(x_ref, g_ref, w_ref, ATTN_SCALE * LOG2E)
    u_ref[...] = u
    n_q = D_ATTN // LANES
    n_grp = buf_a.shape[0]

    def emit(level, r, c, rows):
        dst, c0 = (q_refs[level], c) if c < n_q else (kv_refs[level], c - n_q)
        dst[0, r, :, c0 * LANES:(c0 + 1) * LANES] = rows.astype(jnp.bfloat16)

    for c in range(n_grp):
        src, c0 = (q, c) if c < n_q else (kvf, c - n_q)
        rows = src[:, c0 * LANES:(c0 + 1) * LANES]
        buf_a[c] = rows
        emit(0, 0, c, rows)
    src_buf, dst_buf = buf_a, buf_b
    for level in range(1, n_pat):
        d_prev, dil = DILATIONS[level - 1], DILATIONS[level]
        step = dil // d_prev
        rows_prev, rows_cur = tm // d_prev, tm // dil
        keep = level + 1 < n_pat
        for r in range(dil):
            r_lo, r_hi = r % d_prev, r // d_prev
            for c in range(n_grp):
                rows = src_buf[c, pl.ds(r_lo * rows_prev + r_hi, rows_cur, stride=step), :]
                if keep:
                    dst_buf[c, r * rows_cur:(r + 1) * rows_cur, :] = rows
                emit(level, r, c, rows)
        src_buf, dst_buf = dst_buf, src_buf

    @pl.when(pl.program_id(1) >= first_kept)
    def _():
        kvt_ref[0] = kvf.T


def _inproj_prompt(x, g, w, layer, kvt_all, sample_q, sample_kv, cache_t, tm, seq, keep):
    n, d = x.shape
    depth, _, d_in = w.shape
    d_pool = d_in - 3 * D_ATTN
    b, bps = n // seq, seq // tm
    nb = sample_q.shape[0]
    w_buf = cache_t.shape[-1]
    assert nb == b * bps and w_buf == SPAN * DILATIONS[-1]
    first_kept = (seq - keep) // tm
    one = lambda c: pl.BlockSpec((1, 1, c), lambda bi, j: (bi * bps + j, 0, 0))
    cache_spec = pl.BlockSpec((1, 2, D_ATTN, w_buf), lambda bi, j: (layer * nb + bi * bps + j, 0, 0, 0))
    row = lambda c: pl.BlockSpec((tm, c), lambda bi, j: (bi * bps + j, 0))
    cls = lambda dil, c: pl.BlockSpec((1, dil, tm // dil, c), lambda bi, j: (bi, 0, j, 0))
    cls_shape = lambda dil, c: jax.ShapeDtypeStruct((b, dil, seq // dil, c), jnp.bfloat16)
    kvt_spec = pl.BlockSpec((1, 2 * D_ATTN, tm),
                            lambda bi, j: (layer * b + bi, 0, jnp.maximum(j - first_kept, 0)))
    n_pat = len(DILATIONS)
    n_grp = 3 * D_ATTN // LANES
    outs = pl.pallas_call(
        functools.partial(_inproj_prompt_body, first_kept),
        grid=(b, bps),
        in_specs=[row(d), _layer_spec(g, layer), _layer_spec(w, layer), one(D_ATTN), one(2 * D_ATTN),
                  cache_spec, pl.BlockSpec(memory_space=pl.ANY)],
        out_specs=[cls(dil, D_ATTN) for dil in DILATIONS] + [cls(dil, 2 * D_ATTN) for dil in DILATIONS]
                  + [kvt_spec, row(d_pool), one(D_ATTN)],
        out_shape=[cls_shape(dil, D_ATTN) for dil in DILATIONS]
                  + [cls_shape(dil, 2 * D_ATTN) for dil in DILATIONS]
                  + [jax.ShapeDtypeStruct(kvt_all.shape, jnp.float32),
                     jax.ShapeDtypeStruct((n, d_pool), jnp.float32),
                     jax.ShapeDtypeStruct((nb, 1, D_ATTN), jnp.float32)],
        scratch_shapes=[pltpu.VMEM((n_grp, tm, LANES), jnp.float32)] * 2,
        input_output_aliases={6: 2 * n_pat},
        compiler_params=_params(2),
        name="inproj_prompt",
    )(x, g, w, sample_q.reshape(nb, 1, D_ATTN), sample_kv.reshape(nb, 1, 2 * D_ATTN), cache_t, kvt_all)
    return (outs[:n_pat], outs[n_pat:2 * n_pat], outs[2 * n_pat], outs[2 * n_pat + 1],
            outs[2 * n_pat + 2].reshape(nb, D_ATTN))


def _attn_body(dil, rblocks, q_ref, kc_ref, kp_ref, o_ref, l_ref):
    n = pl.program_id(1)
    qi = lax.broadcasted_iota(jnp.int32, (QBLK, 2 * QBLK), 0)
    kj = lax.broadcasted_iota(jnp.int32, (QBLK, 2 * QBLK), 1) - QBLK
    rel = qi - kj
    band = (rel >= 0) & (rel <= SPAN)
    band_first = band & ((kj >= 0) | (n > 0))
    lane = lax.broadcasted_iota(jnp.int32, (QBLK, LANES), 1)
    low = lane < HEAD_DIM

    def one_block(r, j):
        mask = band_first if j == 0 else band
        q_rows = slice(j * QBLK, (j + 1) * QBLK)
        start = j * QBLK * dil + r
        rows = pl.ds(start, QBLK) if dil == 1 else pl.ds(start, QBLK, stride=dil)
        for p in range(D_ATTN // LANES):
            cols = slice(p * LANES, (p + 1) * LANES)
            vcols = slice(D_ATTN + p * LANES, D_ATTN + (p + 1) * LANES)
            qp = q_ref[0, r, q_rows, cols]
            if j == 0:
                k_prev, v_prev = kp_ref[0, r, :, cols], kp_ref[0, r, :, vcols]
            else:
                p_rows = slice((j - 1) * QBLK, j * QBLK)
                k_prev, v_prev = kc_ref[0, r, p_rows, cols], kc_ref[0, r, p_rows, vcols]
            kk = jnp.concatenate([k_prev, kc_ref[0, r, q_rows, cols]], axis=0)
            vv = jnp.concatenate([v_prev, kc_ref[0, r, q_rows, vcols]], axis=0)
            accs, ms, dens = [], [], []
            for sel in (low, ~low):
                qh = jnp.where(sel, qp, jnp.zeros_like(qp))
                s = _dot_nt(qh, kk)
                s = jnp.where(mask, s, NEG_INF)
                m = jnp.max(s, axis=-1, keepdims=True)
                e = jnp.exp2(s - m)
                ms.append(m)
                dens.append(jnp.sum(e, axis=-1, keepdims=True))
                accs.append(_dot(e.astype(jnp.bfloat16), vv))
            den = jnp.where(low, dens[0], dens[1])
            o_ref[0, p, rows, :] = jnp.where(low, accs[0], accs[1]) / den
            l_ref[0, p, rows, :] = (jnp.where(low, ms[0], ms[1]) + jnp.log2(den)) * LN2

    if dil <= ATTN_UNROLL:
        for r in range(dil):
            for j in range(rblocks):
                one_block(r, j)
    else:
        def classes(i, carry):
            for k in range(ATTN_UNROLL):
                for j in range(rblocks):
                    one_block(i * ATTN_UNROLL + k, j)
            return carry
        lax.fori_loop(0, dil // ATTN_UNROLL, classes, None)


def _attn_prompt(q, kv, dil):
    b, _, sc, _ = q.shape
    rblocks = max(ATTN_UNROLL // dil, 1)
    rows = QBLK * rblocks
    nb = sc // rows
    cur = lambda c: pl.BlockSpec((1, dil, rows, c), lambda bi, n: (bi, 0, n, 0))
    prev = pl.BlockSpec((1, dil, QBLK, 2 * D_ATTN), lambda bi, n: (bi, 0, jnp.maximum(n * rblocks - 1, 0), 0))
    n_pair = D_ATTN // LANES
    out = pl.BlockSpec((1, n_pair, rows * dil, LANES), lambda bi, n: (bi, 0, n, 0))
    return pl.pallas_call(
        functools.partial(_attn_body, dil, rblocks),
        grid=(b, nb),
        in_specs=[cur(D_ATTN), cur(2 * D_ATTN), prev],
        out_specs=[out, out],
        out_shape=[jax.ShapeDtypeStruct((b, n_pair, sc * dil, LANES), jnp.float32)] * 2,
        compiler_params=_params(2),
        name=f"attn_d{dil}",
    )(q, kv, kv)


def _to_column(row):
    return jnp.broadcast_to(row, (LANES, row.shape[1])).T[:, :1]


def _to_row(col):
    return jnp.broadcast_to(col, (col.shape[0], LANES)).T[:1, :]


def _sample_attn_step(q_ref, kvn_ref, c_ref, o_ref):
    w_buf = c_ref.shape[-1]
    pos = lax.broadcasted_iota(jnp.int32, (1, w_buf), 1)
    dist = w_buf - pos
    mult = jnp.zeros((1, w_buf), jnp.float32)
    for dil in DILATIONS:
        mult = mult + ((dist % dil == 0) & (dist <= SPAN * dil)).astype(jnp.float32)
    q_col = _to_column(q_ref[0])
    kvn_col = _to_column(kvn_ref[0])
    rows, news = [], []
    for h in range(N_HEADS):
        acc = None
        for g in range(HEAD_DIM // SUBLANES):
            ch = slice(h * HEAD_DIM + g * SUBLANES, h * HEAD_DIM + (g + 1) * SUBLANES)
            part = c_ref[0, 0, ch, :] * q_col[ch, :]
            acc = part if acc is None else acc + part
        rows.append(jnp.sum(acc, axis=0, keepdims=True))
        hs = slice(h * HEAD_DIM, (h + 1) * HEAD_DIM)
        news.append(jnp.sum(q_col[hs, :] * kvn_col[hs, :], axis=0, keepdims=True))
    s = jnp.where(mult > 0.0, jnp.concatenate(rows, axis=0), NEG_INF)
    s_new = jnp.concatenate(news, axis=0)
    m = jnp.maximum(jnp.max(s, axis=-1, keepdims=True), s_new)
    p = mult * jnp.exp(s - m)
    p_new = float(len(DILATIONS)) * jnp.exp(s_new - m)
    den = jnp.sum(p, axis=-1, keepdims=True) + p_new
    outs = []
    for h in range(N_HEADS):
        hs = slice(h * HEAD_DIM, (h + 1) * HEAD_DIM)
        vs = slice(D_ATTN + h * HEAD_DIM, D_ATTN + (h + 1) * HEAD_DIM)
        pv = jnp.sum(c_ref[0, 1, hs, :] * p[h:h + 1, :], axis=1, keepdims=True)
        outs.append((pv + p_new[h:h + 1, :] * kvn_col[vs, :]) / den[h:h + 1, :])
    o_ref[0] = _to_row(jnp.concatenate(outs, axis=0))


def _pool_project(x, attn, z, pw_ref, ps_ref, wo_ref):
    gdim = pw_ref.shape[1]
    zb = z.astype(jnp.bfloat16)
    pooled = jnp.concatenate(
        [_dot(zb[:, gi * gdim:(gi + 1) * gdim], pw_ref[gi]) for gi in range(pw_ref.shape[0])], axis=1)
    pooled = pooled * ps_ref[...]
    mix = jnp.concatenate([attn.astype(jnp.bfloat16), pooled.astype(jnp.bfloat16)], axis=1)
    return x + _dot(mix, wo_ref[...])


def _mlp_ple_math(x, p, gm_ref, wu_ref, wd_ref, gp_ref, wg_ref, wp_ref):
    xn = _rms(x, gm_ref[...]).astype(jnp.bfloat16)
    h = x
    for c in range(wu_ref.shape[1] // FF_CHUNK):
        a = jnp.maximum(_dot(xn, wu_ref[:, c * FF_CHUNK:(c + 1) * FF_CHUNK]), 0.0)
        h = h + _dot((a * a).astype(jnp.bfloat16), wd_ref[c * FF_CHUNK:(c + 1) * FF_CHUNK, :])
    gate = jax.nn.sigmoid(_dot(_rms(h, gp_ref[...]).astype(jnp.bfloat16), wg_ref[...]))
    return h + _dot(p.astype(jnp.bfloat16), wp_ref[...]) * gate


def _post_prompt_body(blocks_per_seq, with_final, x_ref, o1_ref, l1_ref, o2_ref, l2_ref, o3_ref, l3_ref,
                      u_ref, halo_ref, p_ref, pw_ref, ps_ref, wo_ref, gm_ref, wu_ref, wd_ref, gp_ref,
                      wg_ref, wp_ref, gf_ref, h_ref, *rest):
    ubuf = rest[-1]
    tm = x_ref.shape[0]
    pairs = []
    for p in range(o1_ref.shape[1]):
        l1, l2, l3 = l1_ref[0, p], l2_ref[0, p], l3_ref[0, p]
        m = jnp.maximum(jnp.maximum(l1, l2), l3)
        e1, e2, e3 = jnp.exp(l1 - m), jnp.exp(l2 - m), jnp.exp(l3 - m)
        pairs.append((e1 * o1_ref[0, p] + e2 * o2_ref[0, p] + e3 * o3_ref[0, p]) / (e1 + e2 + e3))
    attn = jnp.concatenate(pairs, axis=1)

    blk = pl.program_id(0) % blocks_per_seq
    u = u_ref[...]
    ubuf[:POOL_HALO, :] = jnp.where(blk > 0, halo_ref[...], 0.0)
    ubuf[POOL_HALO:, :] = u
    gdim = u.shape[1] // len(POOL_WINDOWS)
    pos = blk * tm + lax.broadcasted_iota(jnp.int32, (tm, gdim), 0)
    zs = []
    for gi, w in enumerate(POOL_WINDOWS):
        cols = slice(gi * gdim, (gi + 1) * gdim)
        cur = u[:, cols]
        tot = cur
        for k in range(1, w):
            tot = tot + ubuf[POOL_HALO - k:POOL_HALO - k + tm, cols]
        cnt = jnp.minimum(pos + 1, w).astype(jnp.float32)
        zs.append(tot / cnt - cur)
    z = jnp.concatenate(zs, axis=1)
    h = _pool_project(x_ref[...], attn, z, pw_ref, ps_ref, wo_ref)
    h = _mlp_ple_math(h, p_ref[...], gm_ref, wu_ref, wd_ref, gp_ref, wg_ref, wp_ref)
    h_ref[...] = h
    if with_final:
        rest[0][...] = _rms(h, gf_ref[...])


def _post_prompt(x, pats, u, p, layer, pool_w, pool_scale, w_out, g_mlp, w_up, w_down, g_ple, w_gate, w_ple,
                 g_final, seq, tm, with_final):
    n, d = x.shape
    d_pool = u.shape[1]
    row = lambda c: pl.BlockSpec((tm, c), lambda i: (i, 0))
    halo = pl.BlockSpec((POOL_HALO, d_pool), lambda i: (jnp.maximum(i * (tm // POOL_HALO) - 1, 0), 0))
    flat = [a for pair in pats for a in pair]
    bps = seq // tm
    pat = pl.BlockSpec((1, D_ATTN // LANES, tm, LANES), lambda i: (i // bps, 0, i % bps, 0))
    p_spec = pl.BlockSpec((tm, p.shape[1]), lambda i: (layer * (n // tm) + i, 0))
    stacked = lambda a: _layer_spec(a, layer)
    params = (pool_w, pool_scale, w_out, g_mlp, w_up, w_down, g_ple, w_gate, w_ple)
    n_out = 2 if with_final else 1
    return pl.pallas_call(
        functools.partial(_post_prompt_body, bps, with_final),
        grid=(n // tm,),
        in_specs=[row(d)] + [pat] * 6 + [row(d_pool), halo, p_spec] + [stacked(a) for a in params]
                 + [pl.BlockSpec((1, d), lambda i: (0, 0))],
        out_specs=[row(d)] * n_out,
        out_shape=[jax.ShapeDtypeStruct((n, d), jnp.float32)] * n_out,
        scratch_shapes=[pltpu.VMEM((tm + POOL_HALO, d_pool), jnp.float32)],
        compiler_params=_params(1),
        name="post_prompt",
    )(x, *flat, u, u, p, *params, g_final)


def _mixout_sample_body(x_ref, attn_ref, u_ref, st_ref, pw_ref, ps_ref, wo_ref, h_ref, pool_ref):
    u = u_ref[...]
    n_st = st_ref.shape[0]
    gdim = u.shape[1] // len(POOL_WINDOWS)
    zs = []
    for gi, w in enumerate(POOL_WINDOWS):
        cols = slice(gi * gdim, (gi + 1) * gdim)
        cur = u[:, cols]
        tot = cur
        for k in range(1, w):
            tot = tot + st_ref[n_st - k, :, cols]
        zs.append(tot / float(w) - cur)
    z = jnp.concatenate(zs, axis=1)
    h_ref[...] = _pool_project(x_ref[...], attn_ref[...], z, pw_ref, ps_ref, wo_ref)
    pool_ref[:n_st - 1] = st_ref[1:]
    pool_ref[n_st - 1] = u


def _mixout_sample(x, attn, u, state_t, pool_w, pool_scale, w_out, layer):
    n, d = x.shape
    _, n_st, _, d_pool = state_t.shape
    assert n_st == POOL_WINDOWS[-1] - 1
    full = lambda a: pl.BlockSpec(a.shape, lambda i: (0,) * a.ndim)
    stacked = lambda a: _layer_spec(a, layer)
    return pl.pallas_call(
        _mixout_sample_body,
        grid=(1,),
        in_specs=[full(x), full(attn), full(u), stacked(state_t), stacked(pool_w), stacked(pool_scale),
                  stacked(w_out)],
        out_specs=[pl.BlockSpec((n, d), lambda i: (0, 0)), pl.BlockSpec((n_st, n, d_pool), lambda i: (0, 0, 0))],
        out_shape=[jax.ShapeDtypeStruct((n, d), jnp.float32),
                   jax.ShapeDtypeStruct((n_st, n, d_pool), jnp.float32)],
        compiler_params=_params(1),
        name="mixout_sample",
    )(x, attn, u, state_t, pool_w, pool_scale, w_out)


def _mlp_ple_body(with_final, x_ref, p_ref, gm_ref, wu_ref, wd_ref, gp_ref, wg_ref, wp_ref,
                  gf_ref, h_ref, *y_ref):
    h = _mlp_ple_math(x_ref[...], p_ref[...], gm_ref, wu_ref, wd_ref, gp_ref, wg_ref, wp_ref)
    h_ref[...] = h
    if with_final:
        y_ref[0][...] = _rms(h, gf_ref[...])


def _mlp_ple(x, p, layer, g_mlp, w_up, w_down, g_ple, w_gate, w_ple, g_final, tm, with_final):
    n, d = x.shape
    row = lambda c: pl.BlockSpec((tm, c), lambda i: (i, 0))
    p_spec = pl.BlockSpec((tm, p.shape[1]), lambda i: (layer * (n // tm) + i, 0))
    stacked = lambda a: _layer_spec(a, layer)
    n_out = 2 if with_final else 1
    outs = pl.pallas_call(
        functools.partial(_mlp_ple_body, with_final),
        grid=(n // tm,),
        in_specs=[row(d), p_spec, stacked(g_mlp), stacked(w_up), stacked(w_down), stacked(g_ple),
                  stacked(w_gate), stacked(w_ple), pl.BlockSpec((1, d), lambda i: (0, 0))],
        out_specs=[row(d)] * n_out,
        out_shape=[jax.ShapeDtypeStruct((n, d), jnp.float32)] * n_out,
        compiler_params=_params(1),
        name="mlp_ple",
    )(x, p, g_mlp, w_up, w_down, g_ple, w_gate, w_ple, g_final)
    return outs


def kernel(x_prompt, x_sample, cache_attn_kv, state_pool, p_prompt, p_sample, norm_attn_g, w_in, pool_w,
           pool_scale, w_out, norm_mlp_g, w_up, w_down, ple_norm_g, w_ple_gate, w_ple, final_norm_g):
    b, s, d = x_prompt.shape
    nb_s, t_s, _ = x_sample.shape
    depth = w_in.shape[0]
    d_pool = state_pool.shape[-1]
    n_st = state_pool.shape[2]
    w_buf = cache_attn_kv.shape[2]
    kv_keep = min(SPAN * DILATIONS[-1], s)
    assert t_s == 1 and s % (QBLK * DILATIONS[-1]) == 0
    tm = 512
    bf = lambda a: a.astype(jnp.bfloat16)
    rows = lambda a: a.reshape(depth, 1, -1)

    wi, wo, wu, wd = bf(w_in), bf(w_out), bf(w_up), bf(w_down)
    wg, wp, pw = bf(w_ple_gate), bf(w_ple), bf(pool_w)
    ga, gm, gp, ps = rows(norm_attn_g), rows(norm_mlp_g), rows(ple_norm_g), rows(pool_scale)
    g_final = final_norm_g.reshape(1, -1)

    hp = x_prompt.reshape(b * s, d)
    hs = x_sample.reshape(nb_s, d)
    pp_all = p_prompt.reshape(depth * b * s, -1)
    ps_all = p_sample.reshape(depth * nb_s, -1)
    cache_t = cache_attn_kv.transpose(0, 1, 3, 4, 5, 2).reshape(depth * nb_s, 2, D_ATTN, w_buf)
    state_t = state_pool.transpose(0, 2, 1, 3)
    kvt_all = jnp.zeros((depth * b, 2 * D_ATTN, kv_keep), jnp.float32)
    kv_s, pool_p, pool_s = [], [], []
    yp = ys = None
    for i in range(depth):
        last = i == depth - 1

        sq, kvf, su = _inproj_sample(hs, ga, wi, i)
        qs, kvs, kvt_all, u, sattn = _inproj_prompt(hp, ga, wi, i, kvt_all, sq, kvf, cache_t, tm, s, kv_keep)

        pats = [_attn_prompt(q, kv, dil) for q, kv, dil in zip(qs, kvs, DILATIONS)]
        outs = _post_prompt(hp, pats, u, pp_all, i, pw, ps, wo, gm, wu, wd, gp, wg, wp, g_final, s, tm, last)
        hp = outs[0]
        if last:
            yp = outs[1]
        pool_p.append(u.reshape(b, s, d_pool)[:, s - n_st:])

        hs, pool_new = _mixout_sample(hs, sattn, su, state_t, pw, ps, wo, i)
        outs = _mlp_ple(hs, ps_all, i, gm, wu, wd, gp, wg, wp, g_final, nb_s, last)
        hs = outs[0]
        if last:
            ys = outs[1]
        kv_s.append(kvf.reshape(nb_s, 1, 2, N_HEADS, HEAD_DIM))
        pool_s.append(pool_new)

    kv_prompt = kvt_all.reshape(depth, b, 2, N_HEADS, HEAD_DIM, kv_keep).transpose(0, 1, 5, 2, 3, 4)
    pool_sample = jnp.stack(pool_s).transpose(0, 2, 1, 3)
    return (yp.reshape(b, s, d), ys.reshape(nb_s, 1, d), kv_prompt, jnp.stack(kv_s),
            jnp.stack(pool_p), pool_sample)
```

```python
import functools

import jax
import jax.numpy as jnp
from jax import lax
from jax.experimental import pallas as pl
from jax.experimental.pallas import tpu as pltpu

N_HEADS = 8
HEAD_DIM = 64
D_ATTN = N_HEADS * HEAD_DIM
POOL_WINDOWS = (2, 4, 8, 16)
POOL_HALO = 16
DILATIONS = (1, 4, 16)
SPAN = 128
QBLK = 128
ATTN_UNROLL = 16
LAST_GROUPS = 4
FF_CHUNK = 1024
LANES = 128
SUBLANES = 8
EPS = 1e-6
NEG_INF = -1e30
ATTN_SCALE = HEAD_DIM ** -0.5
LOG2E = 1.4426950408889634
LN2 = 0.6931471805599453
VMEM_LIMIT = 56 * 1024 * 1024


def _rms(x, g):
    return x * lax.rsqrt(jnp.mean(x * x, axis=-1, keepdims=True) + EPS) * g


def _dot(a, b):
    return jnp.dot(a, b, preferred_element_type=jnp.float32)


def _dot_nt(a, b):
    return lax.dot_general(a, b, (((1,), (1,)), ((), ())), preferred_element_type=jnp.float32)


def _layer_spec(stacked, layer):
    nd = stacked.ndim - 1
    return pl.BlockSpec((None,) + stacked.shape[1:], lambda *_: (layer,) + (0,) * nd,
                        pipeline_mode=pl.Buffered(1))


def _params(n_grid):
    return pltpu.CompilerParams(dimension_semantics=("arbitrary",) * n_grid,
                                vmem_limit_bytes=VMEM_LIMIT)


def _project(x_ref, g_ref, w_ref, q_scale):
    xn = _rms(x_ref[...], g_ref[...]).astype(jnp.bfloat16)
    proj = _dot(xn, w_ref[...])
    return proj[:, :D_ATTN] * q_scale, proj[:, D_ATTN:3 * D_ATTN], proj[:, 3 * D_ATTN:]


def _inproj_sample_body(x_ref, g_ref, w_ref, q_ref, kvf_ref, u_ref):
    q, kvf, u = _project(x_ref, g_ref, w_ref, ATTN_SCALE)
    q_ref[...] = q
    kvf_ref[...] = kvf
    u_ref[...] = u


def _inproj_sample(x, g, w, layer):
    n, d = x.shape
    d_in = w.shape[-1]
    d_pool = d_in - 3 * D_ATTN
    full = lambda r, c: pl.BlockSpec((r, c), lambda i: (0, 0))
    return pl.pallas_call(
        _inproj_sample_body,
        grid=(1,),
        in_specs=[full(n, d), _layer_spec(g, layer), _layer_spec(w, layer)],
        out_specs=[full(n, D_ATTN), full(n, 2 * D_ATTN), full(n, d_pool)],
        out_shape=[jax.ShapeDtypeStruct((n, D_ATTN), jnp.float32),
                   jax.ShapeDtypeStruct((n, 2 * D_ATTN), jnp.float32),
                   jax.ShapeDtypeStruct((n, d_pool), jnp.float32)],
        compiler_params=_params(1),
        name="inproj_sample",
    )(x, g, w)


def _inproj_prompt_body(first_kept, x_ref, g_ref, w_ref, sq_ref, skv_ref, cache_ref, kvt_in_ref, *refs):
    del kvt_in_ref
    n_pat = len(DILATIONS)
    q_refs, kv_refs = refs[:n_pat], refs[n_pat:2 * n_pat]
    kvt_ref, u_ref, sattn_ref, buf_a, buf_b = refs[2 * n_pat:]
    tm = x_ref.shape[0]
    _sample_attn_step(sq_ref, skv_ref, cache_ref, sattn_ref)
    q, kvf, u = _project(x_ref, g_ref, w_ref, ATTN_SCALE * LOG2E)
    u_ref[...] = u
    n_q = D_ATTN // LANES
    n_grp = buf_a.shape[0]

    def emit(level, r, c, rows):
        dst, c0 = (q_refs[level], c) if c < n_q else (kv_refs[level], c - n_q)
        dst[0, r, :, c0 * LANES:(c0 + 1) * LANES] = rows.astype(jnp.bfloat16)

    for c in range(n_grp):
        src, c0 = (q, c) if c < n_q else (kvf, c - n_q)
        rows = src[:, c0 * LANES:(c0 + 1) * LANES]
        buf_a[c] = rows
        emit(0, 0, c, rows)
    src_buf, dst_buf = buf_a, buf_b
    for level in range(1, n_pat):
        d_prev, dil = DILATIONS[level - 1], DILATIONS[level]
        step = dil // d_prev
        rows_prev, rows_cur = tm // d_prev, tm // dil
        keep = level + 1 < n_pat
        for r in range(dil):
            r_lo, r_hi = r % d_prev, r // d_prev
            for c in range(n_grp):
                rows = src_buf[c, pl.ds(r_lo * rows_prev + r_hi, rows_cur, stride=step), :]
                if keep:
                    dst_buf[c, r * rows_cur:(r + 1) * rows_cur, :] = rows
                emit(level, r, c, rows)
        src_buf, dst_buf = dst_buf, src_buf

    @pl.when(pl.program_id(1) >= first_kept)
    def _():
        kvt_ref[0] = kvf.T


def _inproj_prompt(x, g, w, layer, kvt_all, sample_q, sample_kv, cache_t, tm, seq, keep):
    n, d = x.shape
    depth, _, d_in = w.shape
    d_pool = d_in - 3 * D_ATTN
    b, bps = n // seq, seq // tm
    nb = sample_q.shape[0]
    w_buf = cache_t.shape[-1]
    assert nb == b * bps and w_buf == SPAN * DILATIONS[-1]
    first_kept = (seq - keep) // tm
    one = lambda c: pl.BlockSpec((1, 1, c), lambda bi, j: (bi * bps + j, 0, 0))
    cache_spec = pl.BlockSpec((1, 2, D_ATTN, w_buf), lambda bi, j: (layer * nb + bi * bps + j, 0, 0, 0))
    row = lambda c: pl.BlockSpec((tm, c), lambda bi, j: (bi * bps + j, 0))
    cls = lambda dil, c: pl.BlockSpec((1, dil, tm // dil, c), lambda bi, j: (bi, 0, j, 0))
    cls_shape = lambda dil, c: jax.ShapeDtypeStruct((b, dil, seq // dil, c), jnp.bfloat16)
    kvt_spec = pl.BlockSpec((1, 2 * D_ATTN, tm),
                            lambda bi, j: (layer * b + bi, 0, jnp.maximum(j - first_kept, 0)))
    n_pat = len(DILATIONS)
    n_grp = 3 * D_ATTN // LANES
    outs = pl.pallas_call(
        functools.partial(_inproj_prompt_body, first_kept),
        grid=(b, bps),
        in_specs=[row(d), _layer_spec(g, layer), _layer_spec(w, layer), one(D_ATTN), one(2 * D_ATTN),
                  cache_spec, pl.BlockSpec(memory_space=pl.ANY)],
        out_specs=[cls(dil, D_ATTN) for dil in DILATIONS] + [cls(dil, 2 * D_ATTN) for dil in DILATIONS]
                  + [kvt_spec, row(d_pool), one(D_ATTN)],
        out_shape=[cls_shape(dil, D_ATTN) for dil in DILATIONS]
                  + [cls_shape(dil, 2 * D_ATTN) for dil in DILATIONS]
                  + [jax.ShapeDtypeStruct(kvt_all.shape, jnp.float32),
                     jax.ShapeDtypeStruct((n, d_pool), jnp.float32),
                     jax.ShapeDtypeStruct((nb, 1, D_ATTN), jnp.float32)],
        scratch_shapes=[pltpu.VMEM((n_grp, tm, LANES), jnp.float32)] * 2,
        input_output_aliases={6: 2 * n_pat},
        compiler_params=_params(2),
        name="inproj_prompt",
    )(x, g, w, sample_q.reshape(nb, 1, D_ATTN), sample_kv.reshape(nb, 1, 2 * D_ATTN), cache_t, kvt_all)
    return (outs[:n_pat], outs[n_pat:2 * n_pat], outs[2 * n_pat], outs[2 * n_pat + 1],
            outs[2 * n_pat + 2].reshape(nb, D_ATTN))


def _attn_body(dil, rblocks, out_groups, q_ref, kc_ref, kp_ref, o_ref, l_ref):
    assert out_groups == 1 or rblocks == 1
    n = pl.program_id(1)
    qi = lax.broadcasted_iota(jnp.int32, (QBLK, 2 * QBLK), 0)
    kj = lax.broadcasted_iota(jnp.int32, (QBLK, 2 * QBLK), 1) - QBLK
    rel = qi - kj
    band = (rel >= 0) & (rel <= SPAN)
    band_first = band & ((kj >= 0) | (n > 0))
    lane = lax.broadcasted_iota(jnp.int32, (QBLK, LANES), 1)
    low = lane < HEAD_DIM

    def one_block(r, j):
        mask = band_first if j == 0 else band
        q_rows = slice(j * QBLK, (j + 1) * QBLK)
        if out_groups == 1:
            start, step = j * QBLK * dil + r, dil
        else:
            start, step = (r % out_groups) * (QBLK * dil // out_groups) + r // out_groups, dil // out_groups
        rows = pl.ds(start, QBLK) if step == 1 else pl.ds(start, QBLK, stride=step)
        for p in range(D_ATTN // LANES):
            cols = slice(p * LANES, (p + 1) * LANES)
            vcols = slice(D_ATTN + p * LANES, D_ATTN + (p + 1) * LANES)
            qp = q_ref[0, r, q_rows, cols]
            if j == 0:
                k_prev, v_prev = kp_ref[0, r, :, cols], kp_ref[0, r, :, vcols]
            else:
                p_rows = slice((j - 1) * QBLK, j * QBLK)
                k_prev, v_prev = kc_ref[0, r, p_rows, cols], kc_ref[0, r, p_rows, vcols]
            kk = jnp.concatenate([k_prev, kc_ref[0, r, q_rows, cols]], axis=0)
            vv = jnp.concatenate([v_prev, kc_ref[0, r, q_rows, vcols]], axis=0)
            accs, ms, dens = [], [], []
            for sel in (low, ~low):
                qh = jnp.where(sel, qp, jnp.zeros_like(qp))
                s = _dot_nt(qh, kk)
                s = jnp.where(mask, s, NEG_INF)
                m = jnp.max(s, axis=-1, keepdims=True)
                e = jnp.exp2(s - m)
                ms.append(m)
                dens.append(jnp.sum(e, axis=-1, keepdims=True))
                accs.append(_dot(e.astype(jnp.bfloat16), vv))
            den = jnp.where(low, dens[0], dens[1])
            o_ref[0, p, rows, :] = jnp.where(low, accs[0], accs[1]) / den
            l_ref[0, p, rows, :] = (jnp.where(low, ms[0], ms[1]) + jnp.log2(den)) * LN2

    if dil <= ATTN_UNROLL:
        for r in range(dil):
            for j in range(rblocks):
                one_block(r, j)
    else:
        def classes(i, carry):
            for k in range(ATTN_UNROLL):
                for j in range(rblocks):
                    one_block(i * ATTN_UNROLL + k, j)
            return carry
        lax.fori_loop(0, dil // ATTN_UNROLL, classes, None)


def _attn_prompt(q, kv, dil, out_groups):
    b, _, sc, _ = q.shape
    rblocks = max(ATTN_UNROLL // dil, 1)
    rows = QBLK * rblocks
    nb = sc // rows
    cur = lambda c: pl.BlockSpec((1, dil, rows, c), lambda bi, n: (bi, 0, n, 0))
    prev = pl.BlockSpec((1, dil, QBLK, 2 * D_ATTN), lambda bi, n: (bi, 0, jnp.maximum(n * rblocks - 1, 0), 0))
    n_pair = D_ATTN // LANES
    out = pl.BlockSpec((1, n_pair, rows * dil, LANES), lambda bi, n: (bi, 0, n, 0))
    return pl.pallas_call(
        functools.partial(_attn_body, dil, rblocks, out_groups),
        grid=(b, nb),
        in_specs=[cur(D_ATTN), cur(2 * D_ATTN), prev],
        out_specs=[out, out],
        out_shape=[jax.ShapeDtypeStruct((b, n_pair, sc * dil, LANES), jnp.float32)] * 2,
        compiler_params=_params(2),
        name=f"attn_d{dil}",
    )(q, kv, kv)


def _to_column(row):
    return jnp.broadcast_to(row, (LANES, row.shape[1])).T[:, :1]


def _to_row(col):
    return jnp.broadcast_to(col, (col.shape[0], LANES)).T[:1, :]


def _sample_attn_step(q_ref, kvn_ref, c_ref, o_ref):
    w_buf = c_ref.shape[-1]
    pos = lax.broadcasted_iota(jnp.int32, (1, w_buf), 1)
    dist = w_buf - pos
    mult = jnp.zeros((1, w_buf), jnp.float32)
    for dil in DILATIONS:
        mult = mult + ((dist % dil == 0) & (dist <= SPAN * dil)).astype(jnp.float32)
    q_col = _to_column(q_ref[0])
    kvn_col = _to_column(kvn_ref[0])
    rows, news = [], []
    for h in range(N_HEADS):
        acc = None
        for g in range(HEAD_DIM // SUBLANES):
            ch = slice(h * HEAD_DIM + g * SUBLANES, h * HEAD_DIM + (g + 1) * SUBLANES)
            part = c_ref[0, 0, ch, :] * q_col[ch, :]
            acc = part if acc is None else acc + part
        rows.append(jnp.sum(acc, axis=0, keepdims=True))
        hs = slice(h * HEAD_DIM, (h + 1) * HEAD_DIM)
        news.append(jnp.sum(q_col[hs, :] * kvn_col[hs, :], axis=0, keepdims=True))
    s = jnp.where(mult > 0.0, jnp.concatenate(rows, axis=0), NEG_INF)
    s_new = jnp.concatenate(news, axis=0)
    m = jnp.maximum(jnp.max(s, axis=-1, keepdims=True), s_new)
    p = mult * jnp.exp(s - m)
    p_new = float(len(DILATIONS)) * jnp.exp(s_new - m)
    den = jnp.sum(p, axis=-1, keepdims=True) + p_new
    outs = []
    for h in range(N_HEADS):
        hs = slice(h * HEAD_DIM, (h + 1) * HEAD_DIM)
        vs = slice(D_ATTN + h * HEAD_DIM, D_ATTN + (h + 1) * HEAD_DIM)
        pv = jnp.sum(c_ref[0, 1, hs, :] * p[h:h + 1, :], axis=1, keepdims=True)
        outs.append((pv + p_new[h:h + 1, :] * kvn_col[vs, :]) / den[h:h + 1, :])
    o_ref[0] = _to_row(jnp.concatenate(outs, axis=0))


def _pool_project(x, attn, z, pw_ref, ps_ref, wo_ref):
    gdim = pw_ref.shape[1]
    zb = z.astype(jnp.bfloat16)
    pooled = jnp.concatenate(
        [_dot(zb[:, gi * gdim:(gi + 1) * gdim], pw_ref[gi]) for gi in range(pw_ref.shape[0])], axis=1)
    pooled = pooled * ps_ref[...]
    mix = jnp.concatenate([attn.astype(jnp.bfloat16), pooled.astype(jnp.bfloat16)], axis=1)
    return x + _dot(mix, wo_ref[...])


def _mlp_ple_math(x, p, gm_ref, wu_ref, wd_ref, gp_ref, wg_ref, wp_ref):
    xn = _rms(x, gm_ref[...]).astype(jnp.bfloat16)
    h = x
    for c in range(wu_ref.shape[1] // FF_CHUNK):
        a = jnp.maximum(_dot(xn, wu_ref[:, c * FF_CHUNK:(c + 1) * FF_CHUNK]), 0.0)
        h = h + _dot((a * a).astype(jnp.bfloat16), wd_ref[c * FF_CHUNK:(c + 1) * FF_CHUNK, :])
    gate = jax.nn.sigmoid(_dot(_rms(h, gp_ref[...]).astype(jnp.bfloat16), wg_ref[...]))
    return h + _dot(p.astype(jnp.bfloat16), wp_ref[...]) * gate


def _post_prompt_body(blocks_per_seq, with_final, x_ref, o1_ref, l1_ref, o2_ref, l2_ref, o3_ref, l3_ref,
                      u_ref, halo_ref, p_ref, pw_ref, ps_ref, wo_ref, gm_ref, wu_ref, wd_ref, gp_ref,
                      wg_ref, wp_ref, gf_ref, h_ref, *rest):
    ubuf, seqbuf = rest[-2:]
    tm = x_ref.shape[0]
    groups = o3_ref.shape[3]

    def in_sequence(ref, slot, p):
        for g in range(groups):
            seqbuf[slot, pl.ds(g, tm // groups, stride=groups), :] = ref[0, p, 0, g]
        return seqbuf[slot]

    pairs = []
    for p in range(o1_ref.shape[1]):
        l1, l2, l3 = l1_ref[0, p], l2_ref[0, p], in_sequence(l3_ref, 2 * p, p)
        o3 = in_sequence(o3_ref, 2 * p + 1, p)
        m = jnp.maximum(jnp.maximum(l1, l2), l3)
        e1, e2, e3 = jnp.exp(l1 - m), jnp.exp(l2 - m), jnp.exp(l3 - m)
        pairs.append((e1 * o1_ref[0, p] + e2 * o2_ref[0, p] + e3 * o3) / (e1 + e2 + e3))
    attn = jnp.concatenate(pairs, axis=1)

    blk = pl.program_id(0) % blocks_per_seq
    u = u_ref[...]
    ubuf[:POOL_HALO, :] = jnp.where(blk > 0, halo_ref[...], 0.0)
    ubuf[POOL_HALO:, :] = u
    gdim = u.shape[1] // len(POOL_WINDOWS)
    pos = blk * tm + lax.broadcasted_iota(jnp.int32, (tm, gdim), 0)
    zs = []
    for gi, w in enumerate(POOL_WINDOWS):
        cols = slice(gi * gdim, (gi + 1) * gdim)
        cur = u[:, cols]
        tot = cur
        for k in range(1, w):
            tot = tot + ubuf[POOL_HALO - k:POOL_HALO - k + tm, cols]
        cnt = jnp.minimum(pos + 1, w).astype(jnp.float32)
        zs.append(tot / cnt - cur)
    z = jnp.concatenate(zs, axis=1)
    h = _pool_project(x_ref[...], attn, z, pw_ref, ps_ref, wo_ref)
    h = _mlp_ple_math(h, p_ref[...], gm_ref, wu_ref, wd_ref, gp_ref, wg_ref, wp_ref)
    h_ref[...] = h
    if with_final:
        rest[0][...] = _rms(h, gf_ref[...])


def _post_prompt(x, pats, u, p, layer, pool_w, pool_scale, w_out, g_mlp, w_up, w_down, g_ple, w_gate, w_ple,
                 g_final, seq, tm, with_final):
    n, d = x.shape
    d_pool = u.shape[1]
    row = lambda c: pl.BlockSpec((tm, c), lambda i: (i, 0))
    halo = pl.BlockSpec((POOL_HALO, d_pool), lambda i: (jnp.maximum(i * (tm // POOL_HALO) - 1, 0), 0))
    bps = seq // tm
    n_pair = D_ATTN // LANES
    pat = pl.BlockSpec((1, n_pair, tm, LANES), lambda i: (i // bps, 0, i % bps, 0))
    run = QBLK * DILATIONS[-1]
    per_run = run // tm
    last = [a.reshape(n // seq, n_pair, seq // run, LAST_GROUPS, run // LAST_GROUPS, LANES) for a in pats[-1]]
    pat_last = pl.BlockSpec((1, n_pair, 1, LAST_GROUPS, tm // LAST_GROUPS, LANES),
                            lambda i: (i // bps, 0, (i % bps) // per_run, 0, i % per_run, 0))
    flat = [a for pair in pats[:-1] for a in pair] + last
    p_spec = pl.BlockSpec((tm, p.shape[1]), lambda i: (layer * (n // tm) + i, 0))
    stacked = lambda a: _layer_spec(a, layer)
    params = (pool_w, pool_scale, w_out, g_mlp, w_up, w_down, g_ple, w_gate, w_ple)
    n_out = 2 if with_final else 1
    return pl.pallas_call(
        functools.partial(_post_prompt_body, bps, with_final),
        grid=(n // tm,),
        in_specs=[row(d)] + [pat] * 4 + [pat_last] * 2 + [row(d_pool), halo, p_spec]
                 + [stacked(a) for a in params] + [pl.BlockSpec((1, d), lambda i: (0, 0))],
        out_specs=[row(d)] * n_out,
        out_shape=[jax.ShapeDtypeStruct((n, d), jnp.float32)] * n_out,
        scratch_shapes=[pltpu.VMEM((tm + POOL_HALO, d_pool), jnp.float32),
                        pltpu.VMEM((2 * n_pair, tm, LANES), jnp.float32)],
        compiler_params=_params(1),
        name="post_prompt",
    )(x, *flat, u, u, p, *params, g_final)


def _mixout_sample_body(x_ref, attn_ref, u_ref, st_ref, pw_ref, ps_ref, wo_ref, h_ref, pool_ref):
    u = u_ref[...]
    n_st = st_ref.shape[0]
    gdim = u.shape[1] // len(POOL_WINDOWS)
    zs = []
    for gi, w in enumerate(POOL_WINDOWS):
        cols = slice(gi * gdim, (gi + 1) * gdim)
        cur = u[:, cols]
        tot = cur
        for k in range(1, w):
            tot = tot + st_ref[n_st - k, :, cols]
        zs.append(tot / float(w) - cur)
    z = jnp.concatenate(zs, axis=1)
    h_ref[...] = _pool_project(x_ref[...], attn_ref[...], z, pw_ref, ps_ref, wo_ref)
    pool_ref[:n_st - 1] = st_ref[1:]
    pool_ref[n_st - 1] = u


def _mixout_sample(x, attn, u, state_t, pool_w, pool_scale, w_out, layer):
    n, d = x.shape
    _, n_st, _, d_pool = state_t.shape
    assert n_st == POOL_WINDOWS[-1] - 1
    full = lambda a: pl.BlockSpec(a.shape, lambda i: (0,) * a.ndim)
    stacked = lambda a: _layer_spec(a, layer)
    return pl.pallas_call(
        _mixout_sample_body,
        grid=(1,),
        in_specs=[full(x), full(attn), full(u), stacked(state_t), stacked(pool_w), stacked(pool_scale),
                  stacked(w_out)],
        out_specs=[pl.BlockSpec((n, d), lambda i: (0, 0)), pl.BlockSpec((n_st, n, d_pool), lambda i: (0, 0, 0))],
        out_shape=[jax.ShapeDtypeStruct((n, d), jnp.float32),
                   jax.ShapeDtypeStruct((n_st, n, d_pool), jnp.float32)],
        compiler_params=_params(1),
        name="mixout_sample",
    )(x, attn, u, state_t, pool_w, pool_scale, w_out)


def _mlp_ple_body(with_final, x_ref, p_ref, gm_ref, wu_ref, wd_ref, gp_ref, wg_ref, wp_ref,
                  gf_ref, h_ref, *y_ref):
    h = _mlp_ple_math(x_ref[...], p_ref[...], gm_ref, wu_ref, wd_ref, gp_ref, wg_ref, wp_ref)
    h_ref[...] = h
    if with_final:
        y_ref[0][...] = _rms(h, gf_ref[...])


def _mlp_ple(x, p, layer, g_mlp, w_up, w_down, g_ple, w_gate, w_ple, g_final, tm, with_final):
    n, d = x.shape
    row = lambda c: pl.BlockSpec((tm, c), lambda i: (i, 0))
    p_spec = pl.BlockSpec((tm, p.shape[1]), lambda i: (layer * (n // tm) + i, 0))
    stacked = lambda a: _layer_spec(a, layer)
    n_out = 2 if with_final else 1
    outs = pl.pallas_call(
        functools.partial(_mlp_ple_body, with_final),
        grid=(n // tm,),
        in_specs=[row(d), p_spec, stacked(g_mlp), stacked(w_up), stacked(w_down), stacked(g_ple),
                  stacked(w_gate), stacked(w_ple), pl.BlockSpec((1, d), lambda i: (0, 0))],
        out_specs=[row(d)] * n_out,
        out_shape=[jax.ShapeDtypeStruct((n, d), jnp.float32)] * n_out,
        compiler_params=_params(1),
        name="mlp_ple",
    )(x, p, g_mlp, w_up, w_down, g_ple, w_gate, w_ple, g_final)
    return outs


def kernel(x_prompt, x_sample, cache_attn_kv, state_pool, p_prompt, p_sample, norm_attn_g, w_in, pool_w,
           pool_scale, w_out, norm_mlp_g, w_up, w_down, ple_norm_g, w_ple_gate, w_ple, final_norm_g):
    b, s, d = x_prompt.shape
    nb_s, t_s, _ = x_sample.shape
    depth = w_in.shape[0]
    d_pool = state_pool.shape[-1]
    n_st = state_pool.shape[2]
    w_buf = cache_attn_kv.shape[2]
    kv_keep = min(SPAN * DILATIONS[-1], s)
    assert t_s == 1 and s % (QBLK * DILATIONS[-1]) == 0
    tm = 512
    bf = lambda a: a.astype(jnp.bfloat16)
    rows = lambda a: a.reshape(depth, 1, -1)

    wi, wo, wu, wd = bf(w_in), bf(w_out), bf(w_up), bf(w_down)
    wg, wp, pw = bf(w_ple_gate), bf(w_ple), bf(pool_w)
    ga, gm, gp, ps = rows(norm_attn_g), rows(norm_mlp_g), rows(ple_norm_g), rows(pool_scale)
    g_final = final_norm_g.reshape(1, -1)

    hp = x_prompt.reshape(b * s, d)
    hs = x_sample.reshape(nb_s, d)
    pp_all = p_prompt.reshape(depth * b * s, -1)
    ps_all = p_sample.reshape(depth * nb_s, -1)
    cache_t = cache_attn_kv.transpose(0, 1, 3, 4, 5, 2).reshape(depth * nb_s, 2, D_ATTN, w_buf)
    state_t = state_pool.transpose(0, 2, 1, 3)
    kvt_all = jnp.zeros((depth * b, 2 * D_ATTN, kv_keep), jnp.float32)
    kv_s, pool_p, pool_s = [], [], []
    yp = ys = None
    for i in range(depth):
        last = i == depth - 1

        sq, kvf, su = _inproj_sample(hs, ga, wi, i)
        qs, kvs, kvt_all, u, sattn = _inproj_prompt(hp, ga, wi, i, kvt_all, sq, kvf, cache_t, tm, s, kv_keep)

        groups = [1] * (len(DILATIONS) - 1) + [LAST_GROUPS]
        pats = [_attn_prompt(q, kv, dil, og) for q, kv, dil, og in zip(qs, kvs, DILATIONS, groups)]
        outs = _post_prompt(hp, pats, u, pp_all, i, pw, ps, wo, gm, wu, wd, gp, wg, wp, g_final, s, tm, last)
        hp = outs[0]
        if last:
            yp = outs[1]
        pool_p.append(u.reshape(b, s, d_pool)[:, s - n_st:])

        hs, pool_new = _mixout_sample(hs, sattn, su, state_t, pw, ps, wo, i)
        outs = _mlp_ple(hs, ps_all, i, gm, wu, wd, gp, wg, wp, g_final, nb_s, last)
        hs = outs[0]
        if last:
            ys = outs[1]
        kv_s.append(kvf.reshape(nb_s, 1, 2, N_HEADS, HEAD_DIM))
        pool_s.append(pool_new)

    kv_prompt = kvt_all.reshape(depth, b, 2, N_HEADS, HEAD_DIM, kv_keep).transpose(0, 1, 5, 2, 3, 4)
    pool_sample = jnp.stack(pool_s).transpose(0, 2, 1, 3)
    return (yp.reshape(b, s, d), ys.reshape(nb_s, 1, d), kv_prompt, jnp.stack(kv_s),
            jnp.stack(pool_p), pool_sample)
```

```python
import functools

import jax
import jax.numpy as jnp
from jax import lax
from jax.experimental import pallas as pl
from jax.experimental.pallas import tpu as pltpu

N_HEADS = 8
HEAD_DIM = 64
D_ATTN = N_HEADS * HEAD_DIM
POOL_WINDOWS = (2, 4, 8, 16)
POOL_HALO = 16
DILATIONS = (1, 4, 16)
SPAN = 128
QBLK = 128
ATTN_UNROLL = 8
LAST_GROUPS = 4
FF_CHUNK = 1024
LANES = 128
SUBLANES = 8
EPS = 1e-6
NEG_INF = -1e30
ATTN_SCALE = HEAD_DIM ** -0.5
LOG2E = 1.4426950408889634
LN2 = 0.6931471805599453
VMEM_LIMIT = 56 * 1024 * 1024


def _rms(x, g):
    return x * lax.rsqrt(jnp.mean(x * x, axis=-1, keepdims=True) + EPS) * g


def _dot(a, b):
    return jnp.dot(a, b, preferred_element_type=jnp.float32)


def _dot_nt(a, b):
    return lax.dot_general(a, b, (((1,), (1,)), ((), ())), preferred_element_type=jnp.float32)


def _layer_spec(stacked, layer):
    nd = stacked.ndim - 1
    return pl.BlockSpec((None,) + stacked.shape[1:], lambda *_: (layer,) + (0,) * nd,
                        pipeline_mode=pl.Buffered(1))


def _params(n_grid):
    return pltpu.CompilerParams(dimension_semantics=("arbitrary",) * n_grid,
                                vmem_limit_bytes=VMEM_LIMIT)


def _project(x_ref, g_ref, w_ref, q_scale):
    xn = _rms(x_ref[...], g_ref[...]).astype(jnp.bfloat16)
    proj = _dot(xn, w_ref[...])
    return proj[:, :D_ATTN] * q_scale, proj[:, D_ATTN:3 * D_ATTN], proj[:, 3 * D_ATTN:]


def _inproj_sample_body(x_ref, g_ref, w_ref, q_ref, kvf_ref, u_ref):
    q, kvf, u = _project(x_ref, g_ref, w_ref, ATTN_SCALE)
    q_ref[...] = q
    kvf_ref[...] = kvf
    u_ref[...] = u


def _inproj_sample(x, g, w, layer):
    n, d = x.shape
    d_in = w.shape[-1]
    d_pool = d_in - 3 * D_ATTN
    full = lambda r, c: pl.BlockSpec((r, c), lambda i: (0, 0))
    return pl.pallas_call(
        _inproj_sample_body,
        grid=(1,),
        in_specs=[full(n, d), _layer_spec(g, layer), _layer_spec(w, layer)],
        out_specs=[full(n, D_ATTN), full(n, 2 * D_ATTN), full(n, d_pool)],
        out_shape=[jax.ShapeDtypeStruct((n, D_ATTN), jnp.float32),
                   jax.ShapeDtypeStruct((n, 2 * D_ATTN), jnp.float32),
                   jax.ShapeDtypeStruct((n, d_pool), jnp.float32)],
        compiler_params=_params(1),
        name="inproj_sample",
    )(x, g, w)


def _inproj_prompt_body(first_kept, x_ref, g_ref, w_ref, sq_ref, skv_ref, cache_ref, kvt_in_ref, *refs):
    del kvt_in_ref
    n_pat = len(DILATIONS)
    q_refs, kv_refs = refs[:n_pat], refs[n_pat:2 * n_pat]
    kvt_ref, u_ref, sattn_ref, buf_a, buf_b = refs[2 * n_pat:]
    tm = x_ref.shape[0]
    _sample_attn_step(sq_ref, skv_ref, cache_ref, sattn_ref)
    q, kvf, u = _project(x_ref, g_ref, w_ref, ATTN_SCALE * LOG2E)
    u_ref[...] = u
    n_q = D_ATTN // LANES
    n_grp = buf_a.shape[0]

    def emit(level, r, c, rows):
        dst, c0 = (q_refs[level], c) if c < n_q else (kv_refs[level], c - n_q)
        dst[0, r, :, c0 * LANES:(c0 + 1) * LANES] = rows.astype(jnp.bfloat16)

    for c in range(n_grp):
        src, c0 = (q, c) if c < n_q else (kvf, c - n_q)
        rows = src[:, c0 * LANES:(c0 + 1) * LANES]
        buf_a[c] = rows
        emit(0, 0, c, rows)
    src_buf, dst_buf = buf_a, buf_b
    for level in range(1, n_pat):
        d_prev, dil = DILATIONS[level - 1], DILATIONS[level]
        step = dil // d_prev
        rows_prev, rows_cur = tm // d_prev, tm // dil
        keep = level + 1 < n_pat
        for r in range(dil):
            r_lo, r_hi = r % d_prev, r // d_prev
            for c in range(n_grp):
                rows = src_buf[c, pl.ds(r_lo * rows_prev + r_hi, rows_cur, stride=step), :]
                if keep:
                    dst_buf[c, r * rows_cur:(r + 1) * rows_cur, :] = rows
                emit(level, r, c, rows)
        src_buf, dst_buf = dst_buf, src_buf

    @pl.when(pl.program_id(1) >= first_kept)
    def _():
        kvt_ref[0] = kvf.T


def _inproj_prompt(x, g, w, layer, kvt_all, sample_q, sample_kv, cache_t, tm, seq, keep):
    n, d = x.shape
    depth, _, d_in = w.shape
    d_pool = d_in - 3 * D_ATTN
    b, bps = n // seq, seq // tm
    nb = sample_q.shape[0]
    w_buf = cache_t.shape[-1]
    assert nb == b * bps and w_buf == SPAN * DILATIONS[-1]
    first_kept = (seq - keep) // tm
    one = lambda c: pl.BlockSpec((1, 1, c), lambda bi, j: (bi * bps + j, 0, 0))
    cache_spec = pl.BlockSpec((1, 2, D_ATTN, w_buf), lambda bi, j: (layer * nb + bi * bps + j, 0, 0, 0))
    row = lambda c: pl.BlockSpec((tm, c), lambda bi, j: (bi * bps + j, 0))
    cls = lambda dil, c: pl.BlockSpec((1, dil, tm // dil, c), lambda bi, j: (bi, 0, j, 0))
    cls_shape = lambda dil, c: jax.ShapeDtypeStruct((b, dil, seq // dil, c), jnp.bfloat16)
    kvt_spec = pl.BlockSpec((1, 2 * D_ATTN, tm),
                            lambda bi, j: (layer * b + bi, 0, jnp.maximum(j - first_kept, 0)))
    n_pat = len(DILATIONS)
    n_grp = 3 * D_ATTN // LANES
    outs = pl.pallas_call(
        functools.partial(_inproj_prompt_body, first_kept),
        grid=(b, bps),
        in_specs=[row(d), _layer_spec(g, layer), _layer_spec(w, layer), one(D_ATTN), one(2 * D_ATTN),
                  cache_spec, pl.BlockSpec(memory_space=pl.ANY)],
        out_specs=[cls(dil, D_ATTN) for dil in DILATIONS] + [cls(dil, 2 * D_ATTN) for dil in DILATIONS]
                  + [kvt_spec, row(d_pool), one(D_ATTN)],
        out_shape=[cls_shape(dil, D_ATTN) for dil in DILATIONS]
                  + [cls_shape(dil, 2 * D_ATTN) for dil in DILATIONS]
                  + [jax.ShapeDtypeStruct(kvt_all.shape, jnp.float32),
                     jax.ShapeDtypeStruct((n, d_pool), jnp.float32),
                     jax.ShapeDtypeStruct((nb, 1, D_ATTN), jnp.float32)],
        scratch_shapes=[pltpu.VMEM((n_grp, tm, LANES), jnp.float32)] * 2,
        input_output_aliases={6: 2 * n_pat},
        compiler_params=_params(2),
        name="inproj_prompt",
    )(x, g, w, sample_q.reshape(nb, 1, D_ATTN), sample_kv.reshape(nb, 1, 2 * D_ATTN), cache_t, kvt_all)
    return (outs[:n_pat], outs[n_pat:2 * n_pat], outs[2 * n_pat], outs[2 * n_pat + 1],
            outs[2 * n_pat + 2].reshape(nb, D_ATTN))


def _attn_body(dil, rblocks, out_groups, q_ref, kc_ref, o_ref, l_ref, kp_ref):
    assert out_groups == 1 or rblocks == 1
    n = pl.program_id(1)
    qi = lax.broadcasted_iota(jnp.int32, (QBLK, 2 * QBLK), 0)
    kj = lax.broadcasted_iota(jnp.int32, (QBLK, 2 * QBLK), 1) - QBLK
    rel = qi - kj
    band = (rel >= 0) & (rel <= SPAN)
    band_first = band & ((kj >= 0) | (n > 0))
    lane = lax.broadcasted_iota(jnp.int32, (QBLK, LANES), 1)
    low = lane < HEAD_DIM

    @pl.when(n == 0)
    def _():
        kp_ref[...] = jnp.zeros(kp_ref.shape, kp_ref.dtype)

    def one_block(r, j):
        mask = band_first if j == 0 else band
        q_rows = slice(j * QBLK, (j + 1) * QBLK)
        if out_groups == 1:
            start, step = j * QBLK * dil + r, dil
        else:
            start, step = (r % out_groups) * (QBLK * dil // out_groups) + r // out_groups, dil // out_groups
        rows = pl.ds(start, QBLK) if step == 1 else pl.ds(start, QBLK, stride=step)
        for p in range(D_ATTN // LANES):
            cols = slice(p * LANES, (p + 1) * LANES)
            vcols = slice(D_ATTN + p * LANES, D_ATTN + (p + 1) * LANES)
            qp = q_ref[0, r, q_rows, cols]
            if j == 0:
                k_prev, v_prev = kp_ref[r, :, cols], kp_ref[r, :, vcols]
            else:
                p_rows = slice((j - 1) * QBLK, j * QBLK)
                k_prev, v_prev = kc_ref[0, r, p_rows, cols], kc_ref[0, r, p_rows, vcols]
            kk = jnp.concatenate([k_prev, kc_ref[0, r, q_rows, cols]], axis=0)
            vv = jnp.concatenate([v_prev, kc_ref[0, r, q_rows, vcols]], axis=0)
            accs, ms, dens = [], [], []
            for sel in (low, ~low):
                qh = jnp.where(sel, qp, jnp.zeros_like(qp))
                s = _dot_nt(qh, kk)
                s = jnp.where(mask, s, NEG_INF)
                m = jnp.max(s, axis=-1, keepdims=True)
                e = jnp.exp2(s - m)
                ms.append(m)
                dens.append(jnp.sum(e, axis=-1, keepdims=True))
                accs.append(_dot(e.astype(jnp.bfloat16), vv))
            den = jnp.where(low, dens[0], dens[1])
            o_ref[0, p, rows, :] = jnp.where(low, accs[0], accs[1]) / den
            l_ref[0, p, rows, :] = (jnp.where(low, ms[0], ms[1]) + jnp.log2(den)) * LN2

    if dil <= ATTN_UNROLL:
        for r in range(dil):
            for j in range(rblocks):
                one_block(r, j)
    else:
        def classes(i, carry):
            for k in range(ATTN_UNROLL):
                for j in range(rblocks):
                    one_block(i * ATTN_UNROLL + k, j)
            return carry
        lax.fori_loop(0, dil // ATTN_UNROLL, classes, None)

    kp_ref[...] = kc_ref[0, :, (rblocks - 1) * QBLK:, :]


def _attn_prompt(q, kv, dil, out_groups):
    b, _, sc, _ = q.shape
    rblocks = max(ATTN_UNROLL // dil, 1)
    rows = QBLK * rblocks
    nb = sc // rows
    cur = lambda c: pl.BlockSpec((1, dil, rows, c), lambda bi, n: (bi, 0, n, 0))
    n_pair = D_ATTN // LANES
    out = pl.BlockSpec((1, n_pair, rows * dil, LANES), lambda bi, n: (bi, 0, n, 0))
    return pl.pallas_call(
        functools.partial(_attn_body, dil, rblocks, out_groups),
        grid=(b, nb),
        in_specs=[cur(D_ATTN), cur(2 * D_ATTN)],
        out_specs=[out, out],
        out_shape=[jax.ShapeDtypeStruct((b, n_pair, sc * dil, LANES), jnp.float32)] * 2,
        scratch_shapes=[pltpu.VMEM((dil, QBLK, 2 * D_ATTN), jnp.bfloat16)],
        compiler_params=_params(2),
        name=f"attn_d{dil}",
    )(q, kv)


def _to_column(row):
    return jnp.broadcast_to(row, (LANES, row.shape[1])).T[:, :1]


def _to_row(col):
    return jnp.broadcast_to(col, (col.shape[0], LANES)).T[:1, :]


def _sample_attn_step(q_ref, kvn_ref, c_ref, o_ref):
    w_buf = c_ref.shape[-1]
    pos = lax.broadcasted_iota(jnp.int32, (1, w_buf), 1)
    dist = w_buf - pos
    mult = jnp.zeros((1, w_buf), jnp.float32)
    for dil in DILATIONS:
        mult = mult + ((dist % dil == 0) & (dist <= SPAN * dil)).astype(jnp.float32)
    q_col = _to_column(q_ref[0])
    kvn_col = _to_column(kvn_ref[0])
    rows, news = [], []
    for h in range(N_HEADS):
        acc = None
        for g in range(HEAD_DIM // SUBLANES):
            ch = slice(h * HEAD_DIM + g * SUBLANES, h * HEAD_DIM + (g + 1) * SUBLANES)
            part = c_ref[0, 0, ch, :] * q_col[ch, :]
            acc = part if acc is None else acc + part
        rows.append(jnp.sum(acc, axis=0, keepdims=True))
        hs = slice(h * HEAD_DIM, (h + 1) * HEAD_DIM)
        news.append(jnp.sum(q_col[hs, :] * kvn_col[hs, :], axis=0, keepdims=True))
    s = jnp.where(mult > 0.0, jnp.concatenate(rows, axis=0), NEG_INF)
    s_new = jnp.concatenate(news, axis=0)
    m = jnp.maximum(jnp.max(s, axis=-1, keepdims=True), s_new)
    p = mult * jnp.exp(s - m)
    p_new = float(len(DILATIONS)) * jnp.exp(s_new - m)
    den = jnp.sum(p, axis=-1, keepdims=True) + p_new
    outs = []
    for h in range(N_HEADS):
        hs = slice(h * HEAD_DIM, (h + 1) * HEAD_DIM)
        vs = slice(D_ATTN + h * HEAD_DIM, D_ATTN + (h + 1) * HEAD_DIM)
        pv = jnp.sum(c_ref[0, 1, hs, :] * p[h:h + 1, :], axis=1, keepdims=True)
        outs.append((pv + p_new[h:h + 1, :] * kvn_col[vs, :]) / den[h:h + 1, :])
    o_ref[0] = _to_row(jnp.concatenate(outs, axis=0))


def _pool_project(x, attn, z, pw_ref, ps_ref, wo_ref):
    gdim = pw_ref.shape[1]
    zb = z.astype(jnp.bfloat16)
    pooled = jnp.concatenate(
        [_dot(zb[:, gi * gdim:(gi + 1) * gdim], pw_ref[gi]) for gi in range(pw_ref.shape[0])], axis=1)
    pooled = pooled * ps_ref[...]
    mix = jnp.concatenate([attn.astype(jnp.bfloat16), pooled.astype(jnp.bfloat16)], axis=1)
    return x + _dot(mix, wo_ref[...])


def _mlp_ple_math(x, p, gm_ref, wu_ref, wd_ref, gp_ref, wg_ref, wp_ref):
    xn = _rms(x, gm_ref[...]).astype(jnp.bfloat16)
    h = x
    for c in range(wu_ref.shape[1] // FF_CHUNK):
        a = jnp.maximum(_dot(xn, wu_ref[:, c * FF_CHUNK:(c + 1) * FF_CHUNK]), 0.0)
        h = h + _dot((a * a).astype(jnp.bfloat16), wd_ref[c * FF_CHUNK:(c + 1) * FF_CHUNK, :])
    gate = jax.nn.sigmoid(_dot(_rms(h, gp_ref[...]).astype(jnp.bfloat16), wg_ref[...]))
    return h + _dot(p.astype(jnp.bfloat16), wp_ref[...]) * gate


def _post_prompt_body(blocks_per_seq, with_final, x_ref, o1_ref, l1_ref, o2_ref, l2_ref, o3_ref, l3_ref,
                      u_ref, halo_ref, p_ref, pw_ref, ps_ref, wo_ref, gm_ref, wu_ref, wd_ref, gp_ref,
                      wg_ref, wp_ref, gf_ref, h_ref, *rest):
    ubuf, seqbuf = rest[-2:]
    tm = x_ref.shape[0]
    groups = o3_ref.shape[3]

    def in_sequence(ref, slot, p):
        for g in range(groups):
            seqbuf[slot, pl.ds(g, tm // groups, stride=groups), :] = ref[0, p, 0, g]
        return seqbuf[slot]

    pairs = []
    for p in range(o1_ref.shape[1]):
        l1, l2, l3 = l1_ref[0, p], l2_ref[0, p], in_sequence(l3_ref, 2 * p, p)
        o3 = in_sequence(o3_ref, 2 * p + 1, p)
        m = jnp.maximum(jnp.maximum(l1, l2), l3)
        e1, e2, e3 = jnp.exp(l1 - m), jnp.exp(l2 - m), jnp.exp(l3 - m)
        pairs.append((e1 * o1_ref[0, p] + e2 * o2_ref[0, p] + e3 * o3) / (e1 + e2 + e3))
    attn = jnp.concatenate(pairs, axis=1)

    blk = pl.program_id(0) % blocks_per_seq
    u = u_ref[...]
    ubuf[:POOL_HALO, :] = jnp.where(blk > 0, halo_ref[...], 0.0)
    ubuf[POOL_HALO:, :] = u
    gdim = u.shape[1] // len(POOL_WINDOWS)
    pos = blk * tm + lax.broadcasted_iota(jnp.int32, (tm, gdim), 0)
    zs = []
    for gi, w in enumerate(POOL_WINDOWS):
        cols = slice(gi * gdim, (gi + 1) * gdim)
        cur = u[:, cols]
        tot = cur
        for k in range(1, w):
            tot = tot + ubuf[POOL_HALO - k:POOL_HALO - k + tm, cols]
        cnt = jnp.minimum(pos + 1, w).astype(jnp.float32)
        zs.append(tot / cnt - cur)
    z = jnp.concatenate(zs, axis=1)
    h = _pool_project(x_ref[...], attn, z, pw_ref, ps_ref, wo_ref)
    h = _mlp_ple_math(h, p_ref[...], gm_ref, wu_ref, wd_ref, gp_ref, wg_ref, wp_ref)
    h_ref[...] = h
    if with_final:
        rest[0][...] = _rms(h, gf_ref[...])


def _post_prompt(x, pats, u, p, layer, pool_w, pool_scale, w_out, g_mlp, w_up, w_down, g_ple, w_gate, w_ple,
                 g_final, seq, tm, with_final):
    n, d = x.shape
    d_pool = u.shape[1]
    row = lambda c: pl.BlockSpec((tm, c), lambda i: (i, 0))
    halo = pl.BlockSpec((POOL_HALO, d_pool), lambda i: (jnp.maximum(i * (tm // POOL_HALO) - 1, 0), 0))
    bps = seq // tm
    n_pair = D_ATTN // LANES
    pat = pl.BlockSpec((1, n_pair, tm, LANES), lambda i: (i // bps, 0, i % bps, 0))
    run = QBLK * DILATIONS[-1]
    per_run = run // tm
    last = [a.reshape(n // seq, n_pair, seq // run, LAST_GROUPS, run // LAST_GROUPS, LANES) for a in pats[-1]]
    pat_last = pl.BlockSpec((1, n_pair, 1, LAST_GROUPS, tm // LAST_GROUPS, LANES),
                            lambda i: (i // bps, 0, (i % bps) // per_run, 0, i % per_run, 0))
    flat = [a for pair in pats[:-1] for a in pair] + last
    p_spec = pl.BlockSpec((tm, p.shape[1]), lambda i: (layer * (n // tm) + i, 0))
    stacked = lambda a: _layer_spec(a, layer)
    params = (pool_w, pool_scale, w_out, g_mlp, w_up, w_down, g_ple, w_gate, w_ple)
    n_out = 2 if with_final else 1
    return pl.pallas_call(
        functools.partial(_post_prompt_body, bps, with_final),
        grid=(n // tm,),
        in_specs=[row(d)] + [pat] * 4 + [pat_last] * 2 + [row(d_pool), halo, p_spec]
                 + [stacked(a) for a in params] + [pl.BlockSpec((1, d), lambda i: (0, 0))],
        out_specs=[row(d)] * n_out,
        out_shape=[jax.ShapeDtypeStruct((n, d), jnp.float32)] * n_out,
        scratch_shapes=[pltpu.VMEM((tm + POOL_HALO, d_pool), jnp.float32),
                        pltpu.VMEM((2 * n_pair, tm, LANES), jnp.float32)],
        compiler_params=_params(1),
        name="post_prompt",
    )(x, *flat, u, u, p, *params, g_final)


def _mixout_sample_body(x_ref, attn_ref, u_ref, st_ref, pw_ref, ps_ref, wo_ref, h_ref, pool_ref):
    u = u_ref[...]
    n_st = st_ref.shape[0]
    gdim = u.shape[1] // len(POOL_WINDOWS)
    zs = []
    for gi, w in enumerate(POOL_WINDOWS):
        cols = slice(gi * gdim, (gi + 1) * gdim)
        cur = u[:, cols]
        tot = cur
        for k in range(1, w):
            tot = tot + st_ref[n_st - k, :, cols]
        zs.append(tot / float(w) - cur)
    z = jnp.concatenate(zs, axis=1)
    h_ref[...] = _pool_project(x_ref[...], attn_ref[...], z, pw_ref, ps_ref, wo_ref)
    pool_ref[:n_st - 1] = st_ref[1:]
    pool_ref[n_st - 1] = u


def _mixout_sample(x, attn, u, state_t, pool_w, pool_scale, w_out, layer):
    n, d = x.shape
    _, n_st, _, d_pool = state_t.shape
    assert n_st == POOL_WINDOWS[-1] - 1
    full = lambda a: pl.BlockSpec(a.shape, lambda i: (0,) * a.ndim)
    stacked = lambda a: _layer_spec(a, layer)
    return pl.pallas_call(
        _mixout_sample_body,
        grid=(1,),
        in_specs=[full(x), full(attn), full(u), stacked(state_t), stacked(pool_w), stacked(pool_scale),
                  stacked(w_out)],
        out_specs=[pl.BlockSpec((n, d), lambda i: (0, 0)), pl.BlockSpec((n_st, n, d_pool), lambda i: (0, 0, 0))],
        out_shape=[jax.ShapeDtypeStruct((n, d), jnp.float32),
                   jax.ShapeDtypeStruct((n_st, n, d_pool), jnp.float32)],
        compiler_params=_params(1),
        name="mixout_sample",
    )(x, attn, u, state_t, pool_w, pool_scale, w_out)


def _mlp_ple_body(with_final, x_ref, p_ref, gm_ref, wu_ref, wd_ref, gp_ref, wg_ref, wp_ref,
                  gf_ref, h_ref, *y_ref):
    h = _mlp_ple_math(x_ref[...], p_ref[...], gm_ref, wu_ref, wd_ref, gp_ref, wg_ref, wp_ref)
    h_ref[...] = h
    if with_final:
        y_ref[0][...] = _rms(h, gf_ref[...])


def _mlp_ple(x, p, layer, g_mlp, w_up, w_down, g_ple, w_gate, w_ple, g_final, tm, with_final):
    n, d = x.shape
    row = lambda c: pl.BlockSpec((tm, c), lambda i: (i, 0))
    p_spec = pl.BlockSpec((tm, p.shape[1]), lambda i: (layer * (n // tm) + i, 0))
    stacked = lambda a: _layer_spec(a, layer)
    n_out = 2 if with_final else 1
    outs = pl.pallas_call(
        functools.partial(_mlp_ple_body, with_final),
        grid=(n // tm,),
        in_specs=[row(d), p_spec, stacked(g_mlp), stacked(w_up), stacked(w_down), stacked(g_ple),
                  stacked(w_gate), stacked(w_ple), pl.BlockSpec((1, d), lambda i: (0, 0))],
        out_specs=[row(d)] * n_out,
        out_shape=[jax.ShapeDtypeStruct((n, d), jnp.float32)] * n_out,
        compiler_params=_params(1),
        name="mlp_ple",
    )(x, p, g_mlp, w_up, w_down, g_ple, w_gate, w_ple, g_final)
    return outs


def kernel(x_prompt, x_sample, cache_attn_kv, state_pool, p_prompt, p_sample, norm_attn_g, w_in, pool_w,
           pool_scale, w_out, norm_mlp_g, w_up, w_down, ple_norm_g, w_ple_gate, w_ple, final_norm_g):
    b, s, d = x_prompt.shape
    nb_s, t_s, _ = x_sample.shape
    depth = w_in.shape[0]
    d_pool = state_pool.shape[-1]
    n_st = state_pool.shape[2]
    w_buf = cache_attn_kv.shape[2]
    kv_keep = min(SPAN * DILATIONS[-1], s)
    assert t_s == 1 and s % (QBLK * DILATIONS[-1]) == 0
    tm = 512
    bf = lambda a: a.astype(jnp.bfloat16)
    rows = lambda a: a.reshape(depth, 1, -1)

    wi, wo, wu, wd = bf(w_in), bf(w_out), bf(w_up), bf(w_down)
    wg, wp, pw = bf(w_ple_gate), bf(w_ple), bf(pool_w)
    ga, gm, gp, ps = rows(norm_attn_g), rows(norm_mlp_g), rows(ple_norm_g), rows(pool_scale)
    g_final = final_norm_g.reshape(1, -1)

    hp = x_prompt.reshape(b * s, d)
    hs = x_sample.reshape(nb_s, d)
    pp_all = p_prompt.reshape(depth * b * s, -1)
    ps_all = p_sample.reshape(depth * nb_s, -1)
    cache_t = cache_attn_kv.transpose(0, 1, 3, 4, 5, 2).reshape(depth * nb_s, 2, D_ATTN, w_buf)
    state_t = state_pool.transpose(0, 2, 1, 3)
    kvt_all = jnp.zeros((depth * b, 2 * D_ATTN, kv_keep), jnp.float32)
    kv_s, pool_p, pool_s = [], [], []
    yp = ys = None
    for i in range(depth):
        last = i == depth - 1

        sq, kvf, su = _inproj_sample(hs, ga, wi, i)
        qs, kvs, kvt_all, u, sattn = _inproj_prompt(hp, ga, wi, i, kvt_all, sq, kvf, cache_t, tm, s, kv_keep)

        groups = [1] * (len(DILATIONS) - 1) + [LAST_GROUPS]
        pats = [_attn_prompt(q, kv, dil, og) for q, kv, dil, og in zip(qs, kvs, DILATIONS, groups)]
        outs = _post_prompt(hp, pats, u, pp_all, i, pw, ps, wo, gm, wu, wd, gp, wg, wp, g_final, s, tm, last)
        hp = outs[0]
        if last:
            yp = outs[1]
        pool_p.append(u.reshape(b, s, d_pool)[:, s - n_st:])

        hs, pool_new = _mixout_sample(hs, sattn, su, state_t, pw, ps, wo, i)
        outs = _mlp_ple(hs, ps_all, i, gm, wu, wd, gp, wg, wp, g_final, nb_s, last)
        hs = outs[0]
        if last:
            ys = outs[1]
        kv_s.append(kvf.reshape(nb_s, 1, 2, N_HEADS, HEAD_DIM))
        pool_s.append(pool_new)

    kv_prompt = kvt_all.reshape(depth, b, 2, N_HEADS, HEAD_DIM, kv_keep).transpose(0, 1, 5, 2, 3, 4)
    pool_sample = jnp.stack(pool_s).transpose(0, 2, 1, 3)
    return (yp.reshape(b, s, d), ys.reshape(nb_s, 1, d), kv_prompt, jnp.stack(kv_s),
            jnp.stack(pool_p), pool_sample)
```

```python
import functools

import jax
import jax.numpy as jnp
from jax import lax
from jax.experimental import pallas as pl
from jax.experimental.pallas import tpu as pltpu

N_HEADS = 8
HEAD_DIM = 64
D_ATTN = N_HEADS * HEAD_DIM
POOL_WINDOWS = (2, 4, 8, 16)
POOL_HALO = 16
DILATIONS = (1, 4, 16)
SPAN = 128
QBLK = 128
ATTN_UNROLL = 8
LAST_GROUPS = 4
FF_CHUNK = 1024
LANES = 128
SUBLANES = 8
EPS = 1e-6
NEG_INF = -1e30
ATTN_SCALE = HEAD_DIM ** -0.5
LOG2E = 1.4426950408889634
LN2 = 0.6931471805599453
VMEM_LIMIT = 56 * 1024 * 1024


def _rms(x, g):
    return x * lax.rsqrt(jnp.mean(x * x, axis=-1, keepdims=True) + EPS) * g


def _dot(a, b):
    return jnp.dot(a, b, preferred_element_type=jnp.float32)


def _dot_nt(a, b):
    return lax.dot_general(a, b, (((1,), (1,)), ((), ())), preferred_element_type=jnp.float32)


def _layer_spec(stacked, layer):
    nd = stacked.ndim - 1
    return pl.BlockSpec((None,) + stacked.shape[1:], lambda *_: (layer,) + (0,) * nd,
                        pipeline_mode=pl.Buffered(1))


def _params(n_grid):
    return pltpu.CompilerParams(dimension_semantics=("arbitrary",) * n_grid,
                                vmem_limit_bytes=VMEM_LIMIT)


def _project(x_ref, g_ref, w_ref, q_scale):
    xn = _rms(x_ref[...], g_ref[...]).astype(jnp.bfloat16)
    proj = _dot(xn, w_ref[...])
    return proj[:, :D_ATTN] * q_scale, proj[:, D_ATTN:3 * D_ATTN], proj[:, 3 * D_ATTN:]


def _inproj_sample_body(x_ref, g_ref, w_ref, q_ref, kvf_ref, u_ref):
    q, kvf, u = _project(x_ref, g_ref, w_ref, ATTN_SCALE)
    q_ref[...] = q
    kvf_ref[...] = kvf
    u_ref[...] = u


def _inproj_sample(x, g, w, layer):
    n, d = x.shape
    d_in = w.shape[-1]
    d_pool = d_in - 3 * D_ATTN
    full = lambda r, c: pl.BlockSpec((r, c), lambda i: (0, 0))
    return pl.pallas_call(
        _inproj_sample_body,
        grid=(1,),
        in_specs=[full(n, d), _layer_spec(g, layer), _layer_spec(w, layer)],
        out_specs=[full(n, D_ATTN), full(n, 2 * D_ATTN), full(n, d_pool)],
        out_shape=[jax.ShapeDtypeStruct((n, D_ATTN), jnp.float32),
                   jax.ShapeDtypeStruct((n, 2 * D_ATTN), jnp.float32),
                   jax.ShapeDtypeStruct((n, d_pool), jnp.float32)],
        compiler_params=_params(1),
        name="inproj_sample",
    )(x, g, w)


def _inproj_prompt_body(first_kept, x_ref, g_ref, w_ref, sq_ref, skv_ref, cache_ref, kvt_in_ref, *refs):
    del kvt_in_ref
    n_pat = len(DILATIONS)
    q_refs, kv_refs = refs[:n_pat], refs[n_pat:2 * n_pat]
    kvt_ref, u_ref, sattn_ref, buf_a, buf_b = refs[2 * n_pat:]
    tm = x_ref.shape[0]
    _sample_attn_step(sq_ref, skv_ref, cache_ref, sattn_ref)
    q, kvf, u = _project(x_ref, g_ref, w_ref, ATTN_SCALE * LOG2E)
    u_ref[...] = u
    n_q = D_ATTN // LANES
    n_grp = buf_a.shape[0]

    def emit(level, r, c, rows):
        dst, c0 = (q_refs[level], c) if c < n_q else (kv_refs[level], c - n_q)
        dst[0, r, :, c0 * LANES:(c0 + 1) * LANES] = rows.astype(jnp.bfloat16)

    for c in range(n_grp):
        src, c0 = (q, c) if c < n_q else (kvf, c - n_q)
        rows = src[:, c0 * LANES:(c0 + 1) * LANES]
        buf_a[c] = rows
        emit(0, 0, c, rows)
    src_buf, dst_buf = buf_a, buf_b
    for level in range(1, n_pat):
        d_prev, dil = DILATIONS[level - 1], DILATIONS[level]
        step = dil // d_prev
        rows_prev, rows_cur = tm // d_prev, tm // dil
        keep = level + 1 < n_pat
        for r in range(dil):
            r_lo, r_hi = r % d_prev, r // d_prev
            for c in range(n_grp):
                rows = src_buf[c, pl.ds(r_lo * rows_prev + r_hi, rows_cur, stride=step), :]
                if keep:
                    dst_buf[c, r * rows_cur:(r + 1) * rows_cur, :] = rows
                emit(level, r, c, rows)
        src_buf, dst_buf = dst_buf, src_buf

    @pl.when(pl.program_id(1) >= first_kept)
    def _():
        kvt_ref[0] = kvf.T


def _inproj_prompt(x, g, w, layer, kvt_all, sample_q, sample_kv, cache_t, tm, seq, keep):
    n, d = x.shape
    depth, _, d_in = w.shape
    d_pool = d_in - 3 * D_ATTN
    b, bps = n // seq, seq // tm
    nb = sample_q.shape[0]
    w_buf = cache_t.shape[-1]
    assert nb == b * bps and w_buf == SPAN * DILATIONS[-1]
    first_kept = (seq - keep) // tm
    one = lambda c: pl.BlockSpec((1, 1, c), lambda bi, j: (bi * bps + j, 0, 0))
    cache_spec = pl.BlockSpec((1, 2, D_ATTN, w_buf), lambda bi, j: (layer * nb + bi * bps + j, 0, 0, 0))
    row = lambda c: pl.BlockSpec((tm, c), lambda bi, j: (bi * bps + j, 0))
    cls = lambda dil, c: pl.BlockSpec((1, dil, tm // dil, c), lambda bi, j: (bi, 0, j, 0))
    cls_shape = lambda dil, c: jax.ShapeDtypeStruct((b, dil, seq // dil, c), jnp.bfloat16)
    kvt_spec = pl.BlockSpec((1, 2 * D_ATTN, tm),
                            lambda bi, j: (layer * b + bi, 0, jnp.maximum(j - first_kept, 0)))
    n_pat = len(DILATIONS)
    n_grp = 3 * D_ATTN // LANES
    outs = pl.pallas_call(
        functools.partial(_inproj_prompt_body, first_kept),
        grid=(b, bps),
        in_specs=[row(d), _layer_spec(g, layer), _layer_spec(w, layer), one(D_ATTN), one(2 * D_ATTN),
                  cache_spec, pl.BlockSpec(memory_space=pl.ANY)],
        out_specs=[cls(dil, D_ATTN) for dil in DILATIONS] + [cls(dil, 2 * D_ATTN) for dil in DILATIONS]
                  + [kvt_spec, row(d_pool), one(D_ATTN)],
        out_shape=[cls_shape(dil, D_ATTN) for dil in DILATIONS]
                  + [cls_shape(dil, 2 * D_ATTN) for dil in DILATIONS]
                  + [jax.ShapeDtypeStruct(kvt_all.shape, jnp.float32),
                     jax.ShapeDtypeStruct((n, d_pool), jnp.float32),
                     jax.ShapeDtypeStruct((nb, 1, D_ATTN), jnp.float32)],
        scratch_shapes=[pltpu.VMEM((n_grp, tm, LANES), jnp.float32)] * 2,
        input_output_aliases={6: 2 * n_pat},
        compiler_params=_params(2),
        name="inproj_prompt",
    )(x, g, w, sample_q.reshape(nb, 1, D_ATTN), sample_kv.reshape(nb, 1, 2 * D_ATTN), cache_t, kvt_all)
    return (outs[:n_pat], outs[n_pat:2 * n_pat], outs[2 * n_pat], outs[2 * n_pat + 1],
            outs[2 * n_pat + 2].reshape(nb, D_ATTN))


def _attn_body(dil, rblocks, out_groups, q_ref, kc_ref, kp_ref, o_ref, l_ref):
    assert out_groups == 1 or rblocks == 1
    n = pl.program_id(1)
    qi = lax.broadcasted_iota(jnp.int32, (QBLK, 2 * QBLK), 0)
    kj = lax.broadcasted_iota(jnp.int32, (QBLK, 2 * QBLK), 1) - QBLK
    rel = qi - kj
    band = (rel >= 0) & (rel <= SPAN)
    band_first = band & ((kj >= 0) | (n > 0))
    lane = lax.broadcasted_iota(jnp.int32, (QBLK, LANES), 1)
    low = lane < HEAD_DIM

    def one_block(r, j):
        mask = band_first if j == 0 else band
        q_rows = slice(j * QBLK, (j + 1) * QBLK)
        if out_groups == 1:
            start, step = j * QBLK * dil + r, dil
        else:
            start, step = (r % out_groups) * (QBLK * dil // out_groups) + r // out_groups, dil // out_groups
        rows = pl.ds(start, QBLK) if step == 1 else pl.ds(start, QBLK, stride=step)
        for p in range(D_ATTN // LANES):
            cols = slice(p * LANES, (p + 1) * LANES)
            vcols = slice(D_ATTN + p * LANES, D_ATTN + (p + 1) * LANES)
            qp = q_ref[0, r, q_rows, cols]
            if j == 0:
                k_prev, v_prev = kp_ref[0, r, :, cols], kp_ref[0, r, :, vcols]
            else:
                p_rows = slice((j - 1) * QBLK, j * QBLK)
                k_prev, v_prev = kc_ref[0, r, p_rows, cols], kc_ref[0, r, p_rows, vcols]
            kk = jnp.concatenate([k_prev, kc_ref[0, r, q_rows, cols]], axis=0)
            vv = jnp.concatenate([v_prev, kc_ref[0, r, q_rows, vcols]], axis=0)
            accs, ms, dens = [], [], []
            for sel in (low, ~low):
                qh = jnp.where(sel, qp, jnp.zeros_like(qp))
                s = _dot_nt(qh, kk)
                s = jnp.where(mask, s, NEG_INF)
                m = jnp.max(s, axis=-1, keepdims=True)
                e = jnp.exp2(s - m)
                ms.append(m)
                dens.append(jnp.sum(e, axis=-1, keepdims=True))
                accs.append(_dot(e.astype(jnp.bfloat16), vv))
            den = jnp.where(low, dens[0], dens[1])
            o_ref[0, p, rows, :] = jnp.where(low, accs[0], accs[1]) / den
            l_ref[0, p, rows, :] = (jnp.where(low, ms[0], ms[1]) + jnp.log2(den)) * LN2

    if dil <= ATTN_UNROLL:
        for r in range(dil):
            for j in range(rblocks):
                one_block(r, j)
    else:
        def classes(i, carry):
            for k in range(ATTN_UNROLL):
                for j in range(rblocks):
                    one_block(i * ATTN_UNROLL + k, j)
            return carry
        lax.fori_loop(0, dil // ATTN_UNROLL, classes, None)


def _attn_prompt(q, kv, dil, out_groups):
    b, _, sc, _ = q.shape
    rblocks = max(ATTN_UNROLL // dil, 1)
    rows = QBLK * rblocks
    nb = sc // rows
    cur = lambda c: pl.BlockSpec((1, dil, rows, c), lambda bi, n: (bi, 0, n, 0))
    prev = pl.BlockSpec((1, dil, QBLK, 2 * D_ATTN), lambda bi, n: (bi, 0, jnp.maximum(n * rblocks - 1, 0), 0))
    n_pair = D_ATTN // LANES
    out = pl.BlockSpec((1, n_pair, rows * dil, LANES), lambda bi, n: (bi, 0, n, 0))
    return pl.pallas_call(
        functools.partial(_attn_body, dil, rblocks, out_groups),
        grid=(b, nb),
        in_specs=[cur(D_ATTN), cur(2 * D_ATTN), prev],
        out_specs=[out, out],
        out_shape=[jax.ShapeDtypeStruct((b, n_pair, sc * dil, LANES), jnp.float32)] * 2,
        compiler_params=_params(2),
        name=f"attn_d{dil}",
    )(q, kv, kv)


def _to_column(row):
    return jnp.broadcast_to(row, (LANES, row.shape[1])).T[:, :1]


def _to_row(col):
    return jnp.broadcast_to(col, (col.shape[0], LANES)).T[:1, :]


def _sample_attn_step(q_ref, kvn_ref, c_ref, o_ref):
    w_buf = c_ref.shape[-1]
    pos = lax.broadcasted_iota(jnp.int32, (1, w_buf), 1)
    dist = w_buf - pos
    mult = jnp.zeros((1, w_buf), jnp.float32)
    for dil in DILATIONS:
        mult = mult + ((dist % dil == 0) & (dist <= SPAN * dil)).astype(jnp.float32)
    q_col = _to_column(q_ref[0])
    kvn_col = _to_column(kvn_ref[0])
    rows, news = [], []
    for h in range(N_HEADS):
        acc = None
        for g in range(HEAD_DIM // SUBLANES):
            ch = slice(h * HEAD_DIM + g * SUBLANES, h * HEAD_DIM + (g + 1) * SUBLANES)
            part = c_ref[0, 0, ch, :] * q_col[ch, :]
            acc = part if acc is None else acc + part
        rows.append(jnp.sum(acc, axis=0, keepdims=True))
        hs = slice(h * HEAD_DIM, (h + 1) * HEAD_DIM)
        news.append(jnp.sum(q_col[hs, :] * kvn_col[hs, :], axis=0, keepdims=True))
    s = jnp.where(mult > 0.0, jnp.concatenate(rows, axis=0), NEG_INF)
    s_new = jnp.concatenate(news, axis=0)
    m = jnp.maximum(jnp.max(s, axis=-1, keepdims=True), s_new)
    p = mult * jnp.exp(s - m)
    p_new = float(len(DILATIONS)) * jnp.exp(s_new - m)
    den = jnp.sum(p, axis=-1, keepdims=True) + p_new
    outs = []
    for h in range(N_HEADS):
        hs = slice(h * HEAD_DIM, (h + 1) * HEAD_DIM)
        vs = slice(D_ATTN + h * HEAD_DIM, D_ATTN + (h + 1) * HEAD_DIM)
        pv = jnp.sum(c_ref[0, 1, hs, :] * p[h:h + 1, :], axis=1, keepdims=True)
        outs.append((pv + p_new[h:h + 1, :] * kvn_col[vs, :]) / den[h:h + 1, :])
    o_ref[0] = _to_row(jnp.concatenate(outs, axis=0))


def _pool_project(x, attn, z, pw_ref, ps_ref, wo_ref):
    gdim = pw_ref.shape[1]
    zb = z.astype(jnp.bfloat16)
    pooled = jnp.concatenate(
        [_dot(zb[:, gi * gdim:(gi + 1) * gdim], pw_ref[gi]) for gi in range(pw_ref.shape[0])], axis=1)
    pooled = pooled * ps_ref[...]
    mix = jnp.concatenate([attn.astype(jnp.bfloat16), pooled.astype(jnp.bfloat16)], axis=1)
    return x + _dot(mix, wo_ref[...])


def _mlp_ple_math(x, p, gm_ref, wu_ref, wd_ref, gp_ref, wg_ref, wp_ref):
    xn = _rms(x, gm_ref[...]).astype(jnp.bfloat16)
    h = x
    for c in range(wu_ref.shape[1] // FF_CHUNK):
        a = jnp.maximum(_dot(xn, wu_ref[:, c * FF_CHUNK:(c + 1) * FF_CHUNK]), 0.0)
        h = h + _dot((a * a).astype(jnp.bfloat16), wd_ref[c * FF_CHUNK:(c + 1) * FF_CHUNK, :])
    gate = jax.nn.sigmoid(_dot(_rms(h, gp_ref[...]).astype(jnp.bfloat16), wg_ref[...]))
    return h + _dot(p.astype(jnp.bfloat16), wp_ref[...]) * gate


def _post_prompt_body(blocks_per_seq, with_final, x_ref, o1_ref, l1_ref, o2_ref, l2_ref, o3_ref, l3_ref,
                      u_ref, halo_ref, p_ref, pw_ref, ps_ref, wo_ref, gm_ref, wu_ref, wd_ref, gp_ref,
                      wg_ref, wp_ref, gf_ref, h_ref, *rest):
    ubuf, seqbuf = rest[-2:]
    tm = x_ref.shape[0]
    groups = o3_ref.shape[3]

    def in_sequence(ref, slot, p):
        for g in range(groups):
            seqbuf[slot, pl.ds(g, tm // groups, stride=groups), :] = ref[0, p, 0, g]
        return seqbuf[slot]

    pairs = []
    for p in range(o1_ref.shape[1]):
        l1, l2, l3 = l1_ref[0, p], l2_ref[0, p], in_sequence(l3_ref, 2 * p, p)
        o3 = in_sequence(o3_ref, 2 * p + 1, p)
        m = jnp.maximum(jnp.maximum(l1, l2), l3)
        e1, e2, e3 = jnp.exp(l1 - m), jnp.exp(l2 - m), jnp.exp(l3 - m)
        pairs.append((e1 * o1_ref[0, p] + e2 * o2_ref[0, p] + e3 * o3) / (e1 + e2 + e3))
    attn = jnp.concatenate(pairs, axis=1)

    blk = pl.program_id(0) % blocks_per_seq
    u = u_ref[...]
    ubuf[:POOL_HALO, :] = jnp.where(blk > 0, halo_ref[...], 0.0)
    ubuf[POOL_HALO:, :] = u
    gdim = u.shape[1] // len(POOL_WINDOWS)
    pos = blk * tm + lax.broadcasted_iota(jnp.int32, (tm, gdim), 0)
    zs = []
    for gi, w in enumerate(POOL_WINDOWS):
        cols = slice(gi * gdim, (gi + 1) * gdim)
        cur = u[:, cols]
        tot = cur
        for k in range(1, w):
            tot = tot + ubuf[POOL_HALO - k:POOL_HALO - k + tm, cols]
        cnt = jnp.minimum(pos + 1, w).astype(jnp.float32)
        zs.append(tot / cnt - cur)
    z = jnp.concatenate(zs, axis=1)
    h = _pool_project(x_ref[...], attn, z, pw_ref, ps_ref, wo_ref)
    h = _mlp_ple_math(h, p_ref[...], gm_ref, wu_ref, wd_ref, gp_ref, wg_ref, wp_ref)
    h_ref[...] = h
    if with_final:
        rest[0][...] = _rms(h, gf_ref[...])


def _post_prompt(x, pats, u, p, layer, pool_w, pool_scale, w_out, g_mlp, w_up, w_down, g_ple, w_gate, w_ple,
                 g_final, seq, tm, with_final):
    n, d = x.shape
    d_pool = u.shape[1]
    row = lambda c: pl.BlockSpec((tm, c), lambda i: (i, 0))
    halo = pl.BlockSpec((POOL_HALO, d_pool), lambda i: (jnp.maximum(i * (tm // POOL_HALO) - 1, 0), 0))
    bps = seq // tm
    n_pair = D_ATTN // LANES
    pat = pl.BlockSpec((1, n_pair, tm, LANES), lambda i: (i // bps, 0, i % bps, 0))
    run = QBLK * DILATIONS[-1]
    per_run = run // tm
    last = [a.reshape(n // seq, n_pair, seq // run, LAST_GROUPS, run // LAST_GROUPS, LANES) for a in pats[-1]]
    pat_last = pl.BlockSpec((1, n_pair, 1, LAST_GROUPS, tm // LAST_GROUPS, LANES),
                            lambda i: (i // bps, 0, (i % bps) // per_run, 0, i % per_run, 0))
    flat = [a for pair in pats[:-1] for a in pair] + last
    p_spec = pl.BlockSpec((tm, p.shape[1]), lambda i: (layer * (n // tm) + i, 0))
    stacked = lambda a: _layer_spec(a, layer)
    params = (pool_w, pool_scale, w_out, g_mlp, w_up, w_down, g_ple, w_gate, w_ple)
    n_out = 2 if with_final else 1
    return pl.pallas_call(
        functools.partial(_post_prompt_body, bps, with_final),
        grid=(n // tm,),
        in_specs=[row(d)] + [pat] * 4 + [pat_last] * 2 + [row(d_pool), halo, p_spec]
                 + [stacked(a) for a in params] + [pl.BlockSpec((1, d), lambda i: (0, 0))],
        out_specs=[row(d)] * n_out,
        out_shape=[jax.ShapeDtypeStruct((n, d), jnp.float32)] * n_out,
        scratch_shapes=[pltpu.VMEM((tm + POOL_HALO, d_pool), jnp.float32),
                        pltpu.VMEM((2 * n_pair, tm, LANES), jnp.float32)],
        compiler_params=_params(1),
        name="post_prompt",
    )(x, *flat, u, u, p, *params, g_final)


def _sample_tail_body(with_final, x_ref, attn_ref, u_ref, st_ref, p_ref, pw_ref, ps_ref, wo_ref, gm_ref,
                      wu_ref, wd_ref, gp_ref, wg_ref, wp_ref, gf_ref, h_ref, *rest):
    pool_ref, xn_buf, acc = rest[-3:]
    c = pl.program_id(0)

    @pl.when(c == 0)
    def _():
        u = u_ref[...]
        n_st = st_ref.shape[0]
        gdim = u.shape[1] // len(POOL_WINDOWS)
        zs = []
        for gi, w in enumerate(POOL_WINDOWS):
            cols = slice(gi * gdim, (gi + 1) * gdim)
            cur = u[:, cols]
            tot = cur
            for k in range(1, w):
                tot = tot + st_ref[n_st - k, :, cols]
            zs.append(tot / float(w) - cur)
        h1 = _pool_project(x_ref[...], attn_ref[...], jnp.concatenate(zs, axis=1), pw_ref, ps_ref, wo_ref)
        acc[...] = h1
        xn_buf[...] = _rms(h1, gm_ref[...]).astype(jnp.bfloat16)
        pool_ref[:n_st - 1] = st_ref[1:]
        pool_ref[n_st - 1] = u

    a = jnp.maximum(_dot(xn_buf[...], wu_ref[...]), 0.0)
    acc[...] += _dot((a * a).astype(jnp.bfloat16), wd_ref[...])

    @pl.when(c == pl.num_programs(0) - 1)
    def _():
        h = acc[...]
        gate = jax.nn.sigmoid(_dot(_rms(h, gp_ref[...]).astype(jnp.bfloat16), wg_ref[...]))
        h = h + _dot(p_ref[...].astype(jnp.bfloat16), wp_ref[...]) * gate
        h_ref[...] = h
        if with_final:
            rest[0][...] = _rms(h, gf_ref[...])


def _sample_tail(x, attn, u, state_t, p, layer, pool_w, pool_scale, w_out, g_mlp, w_up, w_down, g_ple, w_gate,
                 w_ple, g_final, with_final):
    n, d = x.shape
    _, n_st, _, d_pool = state_t.shape
    assert n_st == POOL_WINDOWS[-1] - 1
    full = lambda a: pl.BlockSpec(a.shape, lambda c: (0,) * a.ndim)
    stacked = lambda a: _layer_spec(a, layer)
    up_spec = pl.BlockSpec((None, d, FF_CHUNK), lambda c: (layer, 0, c))
    down_spec = pl.BlockSpec((None, FF_CHUNK, d), lambda c: (layer, c, 0))
    p_spec = pl.BlockSpec((n, p.shape[1]), lambda c: (layer, 0))
    n_out = 2 if with_final else 1
    outs = pl.pallas_call(
        functools.partial(_sample_tail_body, with_final),
        grid=(w_up.shape[-1] // FF_CHUNK,),
        in_specs=[full(x), full(attn), full(u), stacked(state_t), p_spec, stacked(pool_w), stacked(pool_scale),
                  stacked(w_out), stacked(g_mlp), up_spec, down_spec, stacked(g_ple), stacked(w_gate),
                  stacked(w_ple), full(g_final)],
        out_specs=[pl.BlockSpec((n, d), lambda c: (0, 0))] * n_out
                  + [pl.BlockSpec((n_st, n, d_pool), lambda c: (0, 0, 0))],
        out_shape=[jax.ShapeDtypeStruct((n, d), jnp.float32)] * n_out
                  + [jax.ShapeDtypeStruct((n_st, n, d_pool), jnp.float32)],
        scratch_shapes=[pltpu.VMEM((n, d), jnp.bfloat16), pltpu.VMEM((n, d), jnp.float32)],
        compiler_params=_params(1),
        name="sample_tail",
    )(x, attn, u, state_t, p, pool_w, pool_scale, w_out, g_mlp, w_up, w_down, g_ple, w_gate, w_ple, g_final)
    return outs[:n_out], outs[n_out]


def kernel(x_prompt, x_sample, cache_attn_kv, state_pool, p_prompt, p_sample, norm_attn_g, w_in, pool_w,
           pool_scale, w_out, norm_mlp_g, w_up, w_down, ple_norm_g, w_ple_gate, w_ple, final_norm_g):
    b, s, d = x_prompt.shape
    nb_s, t_s, _ = x_sample.shape
    depth = w_in.shape[0]
    d_pool = state_pool.shape[-1]
    n_st = state_pool.shape[2]
    w_buf = cache_attn_kv.shape[2]
    kv_keep = min(SPAN * DILATIONS[-1], s)
    assert t_s == 1 and s % (QBLK * DILATIONS[-1]) == 0
    tm = 512
    bf = lambda a: a.astype(jnp.bfloat16)
    rows = lambda a: a.reshape(depth, 1, -1)

    wi, wo, wu, wd = bf(w_in), bf(w_out), bf(w_up), bf(w_down)
    wg, wp, pw = bf(w_ple_gate), bf(w_ple), bf(pool_w)
    ga, gm, gp, ps = rows(norm_attn_g), rows(norm_mlp_g), rows(ple_norm_g), rows(pool_scale)
    g_final = final_norm_g.reshape(1, -1)

    hp = x_prompt.reshape(b * s, d)
    hs = x_sample.reshape(nb_s, d)
    pp_all = p_prompt.reshape(depth * b * s, -1)
    ps_all = p_sample.reshape(depth * nb_s, -1)
    cache_t = cache_attn_kv.transpose(0, 1, 3, 4, 5, 2).reshape(depth * nb_s, 2, D_ATTN, w_buf)
    state_t = state_pool.transpose(0, 2, 1, 3)
    kvt_all = jnp.zeros((depth * b, 2 * D_ATTN, kv_keep), jnp.float32)
    kv_s, pool_p, pool_s = [], [], []
    yp = ys = None
    for i in range(depth):
        last = i == depth - 1

        sq, kvf, su = _inproj_sample(hs, ga, wi, i)
        qs, kvs, kvt_all, u, sattn = _inproj_prompt(hp, ga, wi, i, kvt_all, sq, kvf, cache_t, tm, s, kv_keep)

        groups = [1] * (len(DILATIONS) - 1) + [LAST_GROUPS]
        pats = [_attn_prompt(q, kv, dil, og) for q, kv, dil, og in zip(qs, kvs, DILATIONS, groups)]
        outs = _post_prompt(hp, pats, u, pp_all, i, pw, ps, wo, gm, wu, wd, gp, wg, wp, g_final, s, tm, last)
        hp = outs[0]
        if last:
            yp = outs[1]
        pool_p.append(u.reshape(b, s, d_pool)[:, s - n_st:])

        outs, pool_new = _sample_tail(hs, sattn, su, state_t, ps_all, i, pw, ps, wo, gm, wu, wd, gp, wg, wp,
                                      g_final, last)
        hs = outs[0]
        if last:
            ys = outs[1]
        kv_s.append(kvf.reshape(nb_s, 1, 2, N_HEADS, HEAD_DIM))
        pool_s.append(pool_new)

    kv_prompt = kvt_all.reshape(depth, b, 2, N_HEADS, HEAD_DIM, kv_keep).transpose(0, 1, 5, 2, 3, 4)
    pool_sample = jnp.stack(pool_s).transpose(0, 2, 1, 3)
    return (yp.reshape(b, s, d), ys.reshape(nb_s, 1, d), kv_prompt, jnp.stack(kv_s),
            jnp.stack(pool_p), pool_sample)
```

```python
import functools

import jax
import jax.numpy as jnp
from jax import lax
from jax.experimental import pallas as pl
from jax.experimental.pallas import tpu as pltpu

N_HEADS = 8
HEAD_DIM = 64
D_ATTN = N_HEADS * HEAD_DIM
POOL_WINDOWS = (2, 4, 8, 16)
POOL_HALO = 16
DILATIONS = (1, 4, 16)
SPAN = 128
QBLK = 128
ATTN_UNROLL = 8
LAST_GROUPS = 4
FF_CHUNK = 1024
LANES = 128
SUBLANES = 8
EPS = 1e-6
NEG_INF = -1e30
ATTN_SCALE = HEAD_DIM ** -0.5
LOG2E = 1.4426950408889634
LN2 = 0.6931471805599453
VMEM_LIMIT = 56 * 1024 * 1024


def _rms(x, g):
    return x * lax.rsqrt(jnp.mean(x * x, axis=-1, keepdims=True) + EPS) * g


def _dot(a, b):
    return jnp.dot(a, b, preferred_element_type=jnp.float32)


def _dot_nt(a, b):
    return lax.dot_general(a, b, (((1,), (1,)), ((), ())), preferred_element_type=jnp.float32)


def _layer_spec(stacked, layer):
    nd = stacked.ndim - 1
    return pl.BlockSpec((None,) + stacked.shape[1:], lambda *_: (layer,) + (0,) * nd,
                        pipeline_mode=pl.Buffered(1))


def _params(n_grid):
    return pltpu.CompilerParams(dimension_semantics=("arbitrary",) * n_grid,
                                vmem_limit_bytes=VMEM_LIMIT)


def _project(x_ref, g_ref, w_ref, q_scale):
    xn = _rms(x_ref[...], g_ref[...]).astype(jnp.bfloat16)
    proj = _dot(xn, w_ref[...])
    return proj[:, :D_ATTN] * q_scale, proj[:, D_ATTN:3 * D_ATTN], proj[:, 3 * D_ATTN:]


def _inproj_sample_body(x_ref, g_ref, w_ref, q_ref, kvf_ref, u_ref):
    q, kvf, u = _project(x_ref, g_ref, w_ref, ATTN_SCALE)
    q_ref[...] = q
    kvf_ref[...] = kvf
    u_ref[...] = u


def _inproj_sample(x, g, w, layer):
    n, d = x.shape
    d_in = w.shape[-1]
    d_pool = d_in - 3 * D_ATTN
    full = lambda r, c: pl.BlockSpec((r, c), lambda i: (0, 0))
    return pl.pallas_call(
        _inproj_sample_body,
        grid=(1,),
        in_specs=[full(n, d), _layer_spec(g, layer), _layer_spec(w, layer)],
        out_specs=[full(n, D_ATTN), full(n, 2 * D_ATTN), full(n, d_pool)],
        out_shape=[jax.ShapeDtypeStruct((n, D_ATTN), jnp.float32),
                   jax.ShapeDtypeStruct((n, 2 * D_ATTN), jnp.float32),
                   jax.ShapeDtypeStruct((n, d_pool), jnp.float32)],
        compiler_params=_params(1),
        name="inproj_sample",
    )(x, g, w)


def _inproj_prompt_body(first_kept, x_ref, g_ref, w_ref, sq_ref, skv_ref, cache_ref, kvt_in_ref, *refs):
    del kvt_in_ref
    n_pat = len(DILATIONS)
    q_refs, kv_refs = refs[:n_pat], refs[n_pat:2 * n_pat]
    kvt_ref, z_ref, utail_ref, sattn_ref, buf_a, buf_b, ubuf, halo = refs[2 * n_pat:]
    tm = x_ref.shape[0]
    _sample_attn_step(sq_ref, skv_ref, cache_ref, sattn_ref)
    q, kvf, u = _project(x_ref, g_ref, w_ref, ATTN_SCALE * LOG2E)

    blk = pl.program_id(1)
    ubuf[:POOL_HALO, :] = jnp.where(blk > 0, halo[...], 0.0)
    ubuf[POOL_HALO:, :] = u
    gdim = u.shape[1] // len(POOL_WINDOWS)
    pos = blk * tm + lax.broadcasted_iota(jnp.int32, (tm, gdim), 0)
    for gi, w in enumerate(POOL_WINDOWS):
        cols = slice(gi * gdim, (gi + 1) * gdim)
        cur = u[:, cols]
        tot = cur
        for k in range(1, w):
            tot = tot + ubuf[POOL_HALO - k:POOL_HALO - k + tm, cols]
        cnt = jnp.minimum(pos + 1, w).astype(jnp.float32)
        z_ref[:, cols] = (tot / cnt - cur).astype(jnp.bfloat16)
    halo[...] = u[tm - POOL_HALO:, :]
    utail_ref[0] = u[tm - POOL_HALO:, :]
    n_q = D_ATTN // LANES
    n_grp = buf_a.shape[0]

    def emit(level, r, c, rows):
        dst, c0 = (q_refs[level], c) if c < n_q else (kv_refs[level], c - n_q)
        dst[0, r, :, c0 * LANES:(c0 + 1) * LANES] = rows.astype(jnp.bfloat16)

    for c in range(n_grp):
        src, c0 = (q, c) if c < n_q else (kvf, c - n_q)
        rows = src[:, c0 * LANES:(c0 + 1) * LANES]
        buf_a[c] = rows
        emit(0, 0, c, rows)
    src_buf, dst_buf = buf_a, buf_b
    for level in range(1, n_pat):
        d_prev, dil = DILATIONS[level - 1], DILATIONS[level]
        step = dil // d_prev
        rows_prev, rows_cur = tm // d_prev, tm // dil
        keep = level + 1 < n_pat
        for r in range(dil):
            r_lo, r_hi = r % d_prev, r // d_prev
            for c in range(n_grp):
                rows = src_buf[c, pl.ds(r_lo * rows_prev + r_hi, rows_cur, stride=step), :]
                if keep:
                    dst_buf[c, r * rows_cur:(r + 1) * rows_cur, :] = rows
                emit(level, r, c, rows)
        src_buf, dst_buf = dst_buf, src_buf

    @pl.when(pl.program_id(1) >= first_kept)
    def _():
        kvt_ref[0] = kvf.T


def _inproj_prompt(x, g, w, layer, kvt_all, sample_q, sample_kv, cache_t, tm, seq, keep):
    n, d = x.shape
    depth, _, d_in = w.shape
    d_pool = d_in - 3 * D_ATTN
    b, bps = n // seq, seq // tm
    nb = sample_q.shape[0]
    w_buf = cache_t.shape[-1]
    assert nb == b * bps and w_buf == SPAN * DILATIONS[-1]
    first_kept = (seq - keep) // tm
    one = lambda c: pl.BlockSpec((1, 1, c), lambda bi, j: (bi * bps + j, 0, 0))
    cache_spec = pl.BlockSpec((1, 2, D_ATTN, w_buf), lambda bi, j: (layer * nb + bi * bps + j, 0, 0, 0))
    row = lambda c: pl.BlockSpec((tm, c), lambda bi, j: (bi * bps + j, 0))
    cls = lambda dil, c: pl.BlockSpec((1, dil, tm // dil, c), lambda bi, j: (bi, 0, j, 0))
    cls_shape = lambda dil, c: jax.ShapeDtypeStruct((b, dil, seq // dil, c), jnp.bfloat16)
    kvt_spec = pl.BlockSpec((1, 2 * D_ATTN, tm),
                            lambda bi, j: (layer * b + bi, 0, jnp.maximum(j - first_kept, 0)))
    n_pat = len(DILATIONS)
    n_grp = 3 * D_ATTN // LANES
    outs = pl.pallas_call(
        functools.partial(_inproj_prompt_body, first_kept),
        grid=(b, bps),
        in_specs=[row(d), _layer_spec(g, layer), _layer_spec(w, layer), one(D_ATTN), one(2 * D_ATTN),
                  cache_spec, pl.BlockSpec(memory_space=pl.ANY)],
        out_specs=[cls(dil, D_ATTN) for dil in DILATIONS] + [cls(dil, 2 * D_ATTN) for dil in DILATIONS]
                  + [kvt_spec, row(d_pool), pl.BlockSpec((1, POOL_HALO, d_pool), lambda bi, j: (bi, 0, 0)),
                     one(D_ATTN)],
        out_shape=[cls_shape(dil, D_ATTN) for dil in DILATIONS]
                  + [cls_shape(dil, 2 * D_ATTN) for dil in DILATIONS]
                  + [jax.ShapeDtypeStruct(kvt_all.shape, jnp.float32),
                     jax.ShapeDtypeStruct((n, d_pool), jnp.bfloat16),
                     jax.ShapeDtypeStruct((b, POOL_HALO, d_pool), jnp.float32),
                     jax.ShapeDtypeStruct((nb, 1, D_ATTN), jnp.float32)],
        scratch_shapes=[pltpu.VMEM((n_grp, tm, LANES), jnp.float32)] * 2
                       + [pltpu.VMEM((tm + POOL_HALO, d_pool), jnp.float32),
                          pltpu.VMEM((POOL_HALO, d_pool), jnp.float32)],
        input_output_aliases={6: 2 * n_pat},
        compiler_params=_params(2),
        name="inproj_prompt",
    )(x, g, w, sample_q.reshape(nb, 1, D_ATTN), sample_kv.reshape(nb, 1, 2 * D_ATTN), cache_t, kvt_all)
    return (outs[:n_pat], outs[n_pat:2 * n_pat], outs[2 * n_pat], outs[2 * n_pat + 1], outs[2 * n_pat + 2],
            outs[2 * n_pat + 3].reshape(nb, D_ATTN))


def _attn_body(dil, rblocks, out_groups, q_ref, kc_ref, kp_ref, o_ref, l_ref):
    assert out_groups == 1 or rblocks == 1
    n = pl.program_id(1)
    qi = lax.broadcasted_iota(jnp.int32, (QBLK, 2 * QBLK), 0)
    kj = lax.broadcasted_iota(jnp.int32, (QBLK, 2 * QBLK), 1) - QBLK
    rel = qi - kj
    band = (rel >= 0) & (rel <= SPAN)
    band_first = band & ((kj >= 0) | (n > 0))
    lane = lax.broadcasted_iota(jnp.int32, (QBLK, LANES), 1)
    low = lane < HEAD_DIM

    def one_block(r, j):
        mask = band_first if j == 0 else band
        q_rows = slice(j * QBLK, (j + 1) * QBLK)
        if out_groups == 1:
            start, step = j * QBLK * dil + r, dil
        else:
            start, step = (r % out_groups) * (QBLK * dil // out_groups) + r // out_groups, dil // out_groups
        rows = pl.ds(start, QBLK) if step == 1 else pl.ds(start, QBLK, stride=step)
        for p in range(D_ATTN // LANES):
            cols = slice(p * LANES, (p + 1) * LANES)
            vcols = slice(D_ATTN + p * LANES, D_ATTN + (p + 1) * LANES)
            qp = q_ref[0, r, q_rows, cols]
            if j == 0:
                k_prev, v_prev = kp_ref[0, r, :, cols], kp_ref[0, r, :, vcols]
            else:
                p_rows = slice((j - 1) * QBLK, j * QBLK)
                k_prev, v_prev = kc_ref[0, r, p_rows, cols], kc_ref[0, r, p_rows, vcols]
            kk = jnp.concatenate([k_prev, kc_ref[0, r, q_rows, cols]], axis=0)
            vv = jnp.concatenate([v_prev, kc_ref[0, r, q_rows, vcols]], axis=0)
            accs, ms, dens = [], [], []
            for sel in (low, ~low):
                qh = jnp.where(sel, qp, jnp.zeros_like(qp))
                s = _dot_nt(qh, kk)
                s = jnp.where(mask, s, NEG_INF)
                m = jnp.max(s, axis=-1, keepdims=True)
                e = jnp.exp2(s - m)
                ms.append(m)
                dens.append(jnp.sum(e, axis=-1, keepdims=True))
                accs.append(_dot(e.astype(jnp.bfloat16), vv))
            den = jnp.where(low, dens[0], dens[1])
            o_ref[0, p, rows, :] = jnp.where(low, accs[0], accs[1]) / den
            l_ref[0, p, rows, :] = (jnp.where(low, ms[0], ms[1]) + jnp.log2(den)) * LN2

    if dil <= ATTN_UNROLL:
        for r in range(dil):
            for j in range(rblocks):
                one_block(r, j)
    else:
        def classes(i, carry):
            for k in range(ATTN_UNROLL):
                for j in range(rblocks):
                    one_block(i * ATTN_UNROLL + k, j)
            return carry
        lax.fori_loop(0, dil // ATTN_UNROLL, classes, None)


def _attn_prompt(q, kv, dil, out_groups):
    b, _, sc, _ = q.shape
    rblocks = max(ATTN_UNROLL // dil, 1)
    rows = QBLK * rblocks
    nb = sc // rows
    cur = lambda c: pl.BlockSpec((1, dil, rows, c), lambda bi, n: (bi, 0, n, 0))
    prev = pl.BlockSpec((1, dil, QBLK, 2 * D_ATTN), lambda bi, n: (bi, 0, jnp.maximum(n * rblocks - 1, 0), 0))
    n_pair = D_ATTN // LANES
    out = pl.BlockSpec((1, n_pair, rows * dil, LANES), lambda bi, n: (bi, 0, n, 0))
    return pl.pallas_call(
        functools.partial(_attn_body, dil, rblocks, out_groups),
        grid=(b, nb),
        in_specs=[cur(D_ATTN), cur(2 * D_ATTN), prev],
        out_specs=[out, out],
        out_shape=[jax.ShapeDtypeStruct((b, n_pair, sc * dil, LANES), jnp.float32)] * 2,
        compiler_params=_params(2),
        name=f"attn_d{dil}",
    )(q, kv, kv)


def _to_column(row):
    return jnp.broadcast_to(row, (LANES, row.shape[1])).T[:, :1]


def _to_row(col):
    return jnp.broadcast_to(col, (col.shape[0], LANES)).T[:1, :]


def _sample_attn_step(q_ref, kvn_ref, c_ref, o_ref):
    w_buf = c_ref.shape[-1]
    pos = lax.broadcasted_iota(jnp.int32, (1, w_buf), 1)
    dist = w_buf - pos
    mult = jnp.zeros((1, w_buf), jnp.float32)
    for dil in DILATIONS:
        mult = mult + ((dist % dil == 0) & (dist <= SPAN * dil)).astype(jnp.float32)
    q_col = _to_column(q_ref[0])
    kvn_col = _to_column(kvn_ref[0])
    rows, news = [], []
    for h in range(N_HEADS):
        acc = None
        for g in range(HEAD_DIM // SUBLANES):
            ch = slice(h * HEAD_DIM + g * SUBLANES, h * HEAD_DIM + (g + 1) * SUBLANES)
            part = c_ref[0, 0, ch, :] * q_col[ch, :]
            acc = part if acc is None else acc + part
        rows.append(jnp.sum(acc, axis=0, keepdims=True))
        hs = slice(h * HEAD_DIM, (h + 1) * HEAD_DIM)
        news.append(jnp.sum(q_col[hs, :] * kvn_col[hs, :], axis=0, keepdims=True))
    s = jnp.where(mult > 0.0, jnp.concatenate(rows, axis=0), NEG_INF)
    s_new = jnp.concatenate(news, axis=0)
    m = jnp.maximum(jnp.max(s, axis=-1, keepdims=True), s_new)
    p = mult * jnp.exp(s - m)
    p_new = float(len(DILATIONS)) * jnp.exp(s_new - m)
    den = jnp.sum(p, axis=-1, keepdims=True) + p_new
    outs = []
    for h in range(N_HEADS):
        hs = slice(h * HEAD_DIM, (h + 1) * HEAD_DIM)
        vs = slice(D_ATTN + h * HEAD_DIM, D_ATTN + (h + 1) * HEAD_DIM)
        pv = jnp.sum(c_ref[0, 1, hs, :] * p[h:h + 1, :], axis=1, keepdims=True)
        outs.append((pv + p_new[h:h + 1, :] * kvn_col[vs, :]) / den[h:h + 1, :])
    o_ref[0] = _to_row(jnp.concatenate(outs, axis=0))


def _pool_project(x, attn, z, pw_ref, ps_ref, wo_ref):
    gdim = pw_ref.shape[1]
    zb = z.astype(jnp.bfloat16)
    pooled = jnp.concatenate(
        [_dot(zb[:, gi * gdim:(gi + 1) * gdim], pw_ref[gi]) for gi in range(pw_ref.shape[0])], axis=1)
    pooled = pooled * ps_ref[...]
    mix = jnp.concatenate([attn.astype(jnp.bfloat16), pooled.astype(jnp.bfloat16)], axis=1)
    return x + _dot(mix, wo_ref[...])


def _mlp_ple_math(x, p, gm_ref, wu_ref, wd_ref, gp_ref, wg_ref, wp_ref):
    xn = _rms(x, gm_ref[...]).astype(jnp.bfloat16)
    h = x
    for c in range(wu_ref.shape[1] // FF_CHUNK):
        a = jnp.maximum(_dot(xn, wu_ref[:, c * FF_CHUNK:(c + 1) * FF_CHUNK]), 0.0)
        h = h + _dot((a * a).astype(jnp.bfloat16), wd_ref[c * FF_CHUNK:(c + 1) * FF_CHUNK, :])
    gate = jax.nn.sigmoid(_dot(_rms(h, gp_ref[...]).astype(jnp.bfloat16), wg_ref[...]))
    return h + _dot(p.astype(jnp.bfloat16), wp_ref[...]) * gate


def _post_prompt_body(with_final, x_ref, o1_ref, l1_ref, o2_ref, l2_ref, o3_ref, l3_ref,
                      z_ref, p_ref, pw_ref, ps_ref, wo_ref, gm_ref, wu_ref, wd_ref, gp_ref,
                      wg_ref, wp_ref, gf_ref, h_ref, *rest):
    seqbuf = rest[-1]
    tm = x_ref.shape[0]
    groups = o3_ref.shape[3]

    def in_sequence(ref, slot, p):
        for g in range(groups):
            seqbuf[slot, pl.ds(g, tm // groups, stride=groups), :] = ref[0, p, 0, g]
        return seqbuf[slot]

    pairs = []
    for p in range(o1_ref.shape[1]):
        l1, l2, l3 = l1_ref[0, p], l2_ref[0, p], in_sequence(l3_ref, 2 * p, p)
        o3 = in_sequence(o3_ref, 2 * p + 1, p)
        m = jnp.maximum(jnp.maximum(l1, l2), l3)
        e1, e2, e3 = jnp.exp(l1 - m), jnp.exp(l2 - m), jnp.exp(l3 - m)
        pairs.append((e1 * o1_ref[0, p] + e2 * o2_ref[0, p] + e3 * o3) / (e1 + e2 + e3))
    attn = jnp.concatenate(pairs, axis=1)

    h = _pool_project(x_ref[...], attn, z_ref[...], pw_ref, ps_ref, wo_ref)
    h = _mlp_ple_math(h, p_ref[...], gm_ref, wu_ref, wd_ref, gp_ref, wg_ref, wp_ref)
    h_ref[...] = h
    if with_final:
        rest[0][...] = _rms(h, gf_ref[...])


def _post_prompt(x, pats, z, p, layer, pool_w, pool_scale, w_out, g_mlp, w_up, w_down, g_ple, w_gate, w_ple,
                 g_final, seq, tm, with_final):
    n, d = x.shape
    d_pool = z.shape[1]
    row = lambda c: pl.BlockSpec((tm, c), lambda i: (i, 0))
    bps = seq // tm
    n_pair = D_ATTN // LANES
    pat = pl.BlockSpec((1, n_pair, tm, LANES), lambda i: (i // bps, 0, i % bps, 0))
    run = QBLK * DILATIONS[-1]
    per_run = run // tm
    last = [a.reshape(n // seq, n_pair, seq // run, LAST_GROUPS, run // LAST_GROUPS, LANES) for a in pats[-1]]
    pat_last = pl.BlockSpec((1, n_pair, 1, LAST_GROUPS, tm // LAST_GROUPS, LANES),
                            lambda i: (i // bps, 0, (i % bps) // per_run, 0, i % per_run, 0))
    flat = [a for pair in pats[:-1] for a in pair] + last
    p_spec = pl.BlockSpec((tm, p.shape[1]), lambda i: (layer * (n // tm) + i, 0))
    stacked = lambda a: _layer_spec(a, layer)
    params = (pool_w, pool_scale, w_out, g_mlp, w_up, w_down, g_ple, w_gate, w_ple)
    n_out = 2 if with_final else 1
    return pl.pallas_call(
        functools.partial(_post_prompt_body, with_final),
        grid=(n // tm,),
        in_specs=[row(d)] + [pat] * 4 + [pat_last] * 2 + [row(d_pool), p_spec]
                 + [stacked(a) for a in params] + [pl.BlockSpec((1, d), lambda i: (0, 0))],
        out_specs=[row(d)] * n_out,
        out_shape=[jax.ShapeDtypeStruct((n, d), jnp.float32)] * n_out,
        scratch_shapes=[pltpu.VMEM((2 * n_pair, tm, LANES), jnp.float32)],
        compiler_params=_params(1),
        name="post_prompt",
    )(x, *flat, z, p, *params, g_final)


def _sample_tail_body(with_final, x_ref, attn_ref, u_ref, st_ref, p_ref, pw_ref, ps_ref, wo_ref, gm_ref,
                      wu_ref, wd_ref, gp_ref, wg_ref, wp_ref, gf_ref, h_ref, *rest):
    pool_ref, xn_buf, acc = rest[-3:]
    c = pl.program_id(0)

    @pl.when(c == 0)
    def _():
        u = u_ref[...]
        n_st = st_ref.shape[0]
        gdim = u.shape[1] // len(POOL_WINDOWS)
        zs = []
        for gi, w in enumerate(POOL_WINDOWS):
            cols = slice(gi * gdim, (gi + 1) * gdim)
            cur = u[:, cols]
            tot = cur
            for k in range(1, w):
                tot = tot + st_ref[n_st - k, :, cols]
            zs.append(tot / float(w) - cur)
        h1 = _pool_project(x_ref[...], attn_ref[...], jnp.concatenate(zs, axis=1), pw_ref, ps_ref, wo_ref)
        acc[...] = h1
        xn_buf[...] = _rms(h1, gm_ref[...]).astype(jnp.bfloat16)
        pool_ref[:n_st - 1] = st_ref[1:]
        pool_ref[n_st - 1] = u

    a = jnp.maximum(_dot(xn_buf[...], wu_ref[...]), 0.0)
    acc[...] += _dot((a * a).astype(jnp.bfloat16), wd_ref[...])

    @pl.when(c == pl.num_programs(0) - 1)
    def _():
        h = acc[...]
        gate = jax.nn.sigmoid(_dot(_rms(h, gp_ref[...]).astype(jnp.bfloat16), wg_ref[...]))
        h = h + _dot(p_ref[...].astype(jnp.bfloat16), wp_ref[...]) * gate
        h_ref[...] = h
        if with_final:
            rest[0][...] = _rms(h, gf_ref[...])


def _sample_tail(x, attn, u, state_t, p, layer, pool_w, pool_scale, w_out, g_mlp, w_up, w_down, g_ple, w_gate,
                 w_ple, g_final, with_final):
    n, d = x.shape
    _, n_st, _, d_pool = state_t.shape
    assert n_st == POOL_WINDOWS[-1] - 1
    full = lambda a: pl.BlockSpec(a.shape, lambda c: (0,) * a.ndim)
    stacked = lambda a: _layer_spec(a, layer)
    up_spec = pl.BlockSpec((None, d, FF_CHUNK), lambda c: (layer, 0, c))
    down_spec = pl.BlockSpec((None, FF_CHUNK, d), lambda c: (layer, c, 0))
    p_spec = pl.BlockSpec((n, p.shape[1]), lambda c: (layer, 0))
    n_out = 2 if with_final else 1
    outs = pl.pallas_call(
        functools.partial(_sample_tail_body, with_final),
        grid=(w_up.shape[-1] // FF_CHUNK,),
        in_specs=[full(x), full(attn), full(u), stacked(state_t), p_spec, stacked(pool_w), stacked(pool_scale),
                  stacked(w_out), stacked(g_mlp), up_spec, down_spec, stacked(g_ple), stacked(w_gate),
                  stacked(w_ple), full(g_final)],
        out_specs=[pl.BlockSpec((n, d), lambda c: (0, 0))] * n_out
                  + [pl.BlockSpec((n_st, n, d_pool), lambda c: (0, 0, 0))],
        out_shape=[jax.ShapeDtypeStruct((n, d), jnp.float32)] * n_out
                  + [jax.ShapeDtypeStruct((n_st, n, d_pool), jnp.float32)],
        scratch_shapes=[pltpu.VMEM((n, d), jnp.bfloat16), pltpu.VMEM((n, d), jnp.float32)],
        compiler_params=_params(1),
        name="sample_tail",
    )(x, attn, u, state_t, p, pool_w, pool_scale, w_out, g_mlp, w_up, w_down, g_ple, w_gate, w_ple, g_final)
    return outs[:n_out], outs[n_out]


def kernel(x_prompt, x_sample, cache_attn_kv, state_pool, p_prompt, p_sample, norm_attn_g, w_in, pool_w,
           pool_scale, w_out, norm_mlp_g, w_up, w_down, ple_norm_g, w_ple_gate, w_ple, final_norm_g):
    b, s, d = x_prompt.shape
    nb_s, t_s, _ = x_sample.shape
    depth = w_in.shape[0]
    d_pool = state_pool.shape[-1]
    n_st = state_pool.shape[2]
    w_buf = cache_attn_kv.shape[2]
    kv_keep = min(SPAN * DILATIONS[-1], s)
    assert t_s == 1 and s % (QBLK * DILATIONS[-1]) == 0
    tm = 512
    bf = lambda a: a.astype(jnp.bfloat16)
    rows = lambda a: a.reshape(depth, 1, -1)

    wi, wo, wu, wd = bf(w_in), bf(w_out), bf(w_up), bf(w_down)
    wg, wp, pw = bf(w_ple_gate), bf(w_ple), bf(pool_w)
    ga, gm, gp, ps = rows(norm_attn_g), rows(norm_mlp_g), rows(ple_norm_g), rows(pool_scale)
    g_final = final_norm_g.reshape(1, -1)

    hp = x_prompt.reshape(b * s, d)
    hs = x_sample.reshape(nb_s, d)
    pp_all = p_prompt.reshape(depth * b * s, -1)
    ps_all = p_sample.reshape(depth * nb_s, -1)
    cache_t = cache_attn_kv.transpose(0, 1, 3, 4, 5, 2).reshape(depth * nb_s, 2, D_ATTN, w_buf)
    state_t = state_pool.transpose(0, 2, 1, 3)
    kvt_all = jnp.zeros((depth * b, 2 * D_ATTN, kv_keep), jnp.float32)
    kv_s, pool_p, pool_s = [], [], []
    yp = ys = None
    for i in range(depth):
        last = i == depth - 1

        sq, kvf, su = _inproj_sample(hs, ga, wi, i)
        qs, kvs, kvt_all, z, u_tail, sattn = _inproj_prompt(hp, ga, wi, i, kvt_all, sq, kvf, cache_t, tm, s,
                                                            kv_keep)

        groups = [1] * (len(DILATIONS) - 1) + [LAST_GROUPS]
        pats = [_attn_prompt(q, kv, dil, og) for q, kv, dil, og in zip(qs, kvs, DILATIONS, groups)]
        outs = _post_prompt(hp, pats, z, pp_all, i, pw, ps, wo, gm, wu, wd, gp, wg, wp, g_final, s, tm, last)
        hp = outs[0]
        if last:
            yp = outs[1]
        pool_p.append(u_tail[:, POOL_HALO - n_st:])

        outs, pool_new = _sample_tail(hs, sattn, su, state_t, ps_all, i, pw, ps, wo, gm, wu, wd, gp, wg, wp,
                                      g_final, last)
        hs = outs[0]
        if last:
            ys = outs[1]
        kv_s.append(kvf.reshape(nb_s, 1, 2, N_HEADS, HEAD_DIM))
        pool_s.append(pool_new)

    kv_prompt = kvt_all.reshape(depth, b, 2, N_HEADS, HEAD_DIM, kv_keep).transpose(0, 1, 5, 2, 3, 4)
    pool_sample = jnp.stack(pool_s).transpose(0, 2, 1, 3)
    return (yp.reshape(b, s, d), ys.reshape(nb_s, 1, d), kv_prompt, jnp.stack(kv_s),
            jnp.stack(pool_p), pool_sample)
```

```python
import functools

import jax
import jax.numpy as jnp
from jax import lax
from jax.experimental import pallas as pl
from jax.experimental.pallas import tpu as pltpu

N_HEADS = 8
HEAD_DIM = 64
D_ATTN = N_HEADS * HEAD_DIM
POOL_WINDOWS = (2, 4, 8, 16)
POOL_HALO = 16
DILATIONS = (1, 4, 16)
SPAN = 128
QBLK = 128
ATTN_UNROLL = 8
LAST_GROUPS = 4
FF_CHUNK = 1024
LANES = 128
SUBLANES = 8
EPS = 1e-6
NEG_INF = -1e30
ATTN_SCALE = HEAD_DIM ** -0.5
LOG2E = 1.4426950408889634
VMEM_LIMIT = 56 * 1024 * 1024


def _rms(x, g):
    return x * lax.rsqrt(jnp.mean(x * x, axis=-1, keepdims=True) + EPS) * g


def _dot(a, b):
    return jnp.dot(a, b, preferred_element_type=jnp.float32)


def _dot_nt(a, b):
    return lax.dot_general(a, b, (((1,), (1,)), ((), ())), preferred_element_type=jnp.float32)


def _layer_spec(stacked, layer):
    nd = stacked.ndim - 1
    return pl.BlockSpec((None,) + stacked.shape[1:], lambda *_: (layer,) + (0,) * nd,
                        pipeline_mode=pl.Buffered(1))


def _params(n_grid):
    return pltpu.CompilerParams(dimension_semantics=("arbitrary",) * n_grid,
                                vmem_limit_bytes=VMEM_LIMIT)


def _project(x_ref, g_ref, w_ref, q_scale):
    xn = _rms(x_ref[...], g_ref[...]).astype(jnp.bfloat16)
    proj = _dot(xn, w_ref[...])
    return proj[:, :D_ATTN] * q_scale, proj[:, D_ATTN:3 * D_ATTN], proj[:, 3 * D_ATTN:]


def _inproj_sample_body(x_ref, g_ref, w_ref, q_ref, kvf_ref, u_ref):
    q, kvf, u = _project(x_ref, g_ref, w_ref, ATTN_SCALE)
    q_ref[...] = q
    kvf_ref[...] = kvf
    u_ref[...] = u


def _inproj_sample(x, g, w, layer):
    n, d = x.shape
    d_in = w.shape[-1]
    d_pool = d_in - 3 * D_ATTN
    full = lambda r, c: pl.BlockSpec((r, c), lambda i: (0, 0))
    return pl.pallas_call(
        _inproj_sample_body,
        grid=(1,),
        in_specs=[full(n, d), _layer_spec(g, layer), _layer_spec(w, layer)],
        out_specs=[full(n, D_ATTN), full(n, 2 * D_ATTN), full(n, d_pool)],
        out_shape=[jax.ShapeDtypeStruct((n, D_ATTN), jnp.float32),
                   jax.ShapeDtypeStruct((n, 2 * D_ATTN), jnp.float32),
                   jax.ShapeDtypeStruct((n, d_pool), jnp.float32)],
        compiler_params=_params(1),
        name="inproj_sample",
    )(x, g, w)


def _inproj_prompt_body(first_kept, x_ref, g_ref, w_ref, sq_ref, skv_ref, cache_ref, kvt_in_ref, *refs):
    del kvt_in_ref
    n_pat = len(DILATIONS)
    q_refs, kv_refs = refs[:n_pat], refs[n_pat:2 * n_pat]
    kvt_ref, z_ref, utail_ref, sattn_ref, buf_a, buf_b, ubuf, halo = refs[2 * n_pat:]
    tm = x_ref.shape[0]
    _sample_attn_step(sq_ref, skv_ref, cache_ref, sattn_ref)
    q, kvf, u = _project(x_ref, g_ref, w_ref, ATTN_SCALE * LOG2E)

    blk = pl.program_id(1)
    ubuf[:POOL_HALO, :] = jnp.where(blk > 0, halo[...], 0.0)
    ubuf[POOL_HALO:, :] = u
    gdim = u.shape[1] // len(POOL_WINDOWS)
    pos = blk * tm + lax.broadcasted_iota(jnp.int32, (tm, gdim), 0)
    for gi, w in enumerate(POOL_WINDOWS):
        cols = slice(gi * gdim, (gi + 1) * gdim)
        cur = u[:, cols]
        tot = cur
        for k in range(1, w):
            tot = tot + ubuf[POOL_HALO - k:POOL_HALO - k + tm, cols]
        cnt = jnp.minimum(pos + 1, w).astype(jnp.float32)
        z_ref[:, cols] = (tot / cnt - cur).astype(jnp.bfloat16)
    halo[...] = u[tm - POOL_HALO:, :]
    utail_ref[0] = u[tm - POOL_HALO:, :]
    n_q = D_ATTN // LANES
    n_grp = buf_a.shape[0]

    def emit(level, r, c, rows):
        dst, c0 = (q_refs[level], c) if c < n_q else (kv_refs[level], c - n_q)
        dst[0, r, :, c0 * LANES:(c0 + 1) * LANES] = rows.astype(jnp.bfloat16)

    for c in range(n_grp):
        src, c0 = (q, c) if c < n_q else (kvf, c - n_q)
        rows = src[:, c0 * LANES:(c0 + 1) * LANES]
        buf_a[c] = rows
        emit(0, 0, c, rows)
    src_buf, dst_buf = buf_a, buf_b
    for level in range(1, n_pat):
        d_prev, dil = DILATIONS[level - 1], DILATIONS[level]
        step = dil // d_prev
        rows_prev, rows_cur = tm // d_prev, tm // dil
        keep = level + 1 < n_pat
        for r in range(dil):
            r_lo, r_hi = r % d_prev, r // d_prev
            for c in range(n_grp):
                rows = src_buf[c, pl.ds(r_lo * rows_prev + r_hi, rows_cur, stride=step), :]
                if keep:
                    dst_buf[c, r * rows_cur:(r + 1) * rows_cur, :] = rows
                emit(level, r, c, rows)
        src_buf, dst_buf = dst_buf, src_buf

    @pl.when(pl.program_id(1) >= first_kept)
    def _():
        kvt_ref[0] = kvf.T


def _inproj_prompt(x, g, w, layer, kvt_all, sample_q, sample_kv, cache_t, tm, seq, keep):
    n, d = x.shape
    depth, _, d_in = w.shape
    d_pool = d_in - 3 * D_ATTN
    b, bps = n // seq, seq // tm
    nb = sample_q.shape[0]
    w_buf = cache_t.shape[-1]
    assert nb == b * bps and w_buf == SPAN * DILATIONS[-1]
    first_kept = (seq - keep) // tm
    one = lambda c: pl.BlockSpec((1, 1, c), lambda bi, j: (bi * bps + j, 0, 0))
    cache_spec = pl.BlockSpec((1, 2, D_ATTN, w_buf), lambda bi, j: (layer * nb + bi * bps + j, 0, 0, 0))
    row = lambda c: pl.BlockSpec((tm, c), lambda bi, j: (bi * bps + j, 0))
    cls = lambda dil, c: pl.BlockSpec((1, dil, tm // dil, c), lambda bi, j: (bi, 0, j, 0))
    cls_shape = lambda dil, c: jax.ShapeDtypeStruct((b, dil, seq // dil, c), jnp.bfloat16)
    kvt_spec = pl.BlockSpec((1, 2 * D_ATTN, tm),
                            lambda bi, j: (layer * b + bi, 0, jnp.maximum(j - first_kept, 0)))
    n_pat = len(DILATIONS)
    n_grp = 3 * D_ATTN // LANES
    outs = pl.pallas_call(
        functools.partial(_inproj_prompt_body, first_kept),
        grid=(b, bps),
        in_specs=[row(d), _layer_spec(g, layer), _layer_spec(w, layer), one(D_ATTN), one(2 * D_ATTN),
                  cache_spec, pl.BlockSpec(memory_space=pl.ANY)],
        out_specs=[cls(dil, D_ATTN) for dil in DILATIONS] + [cls(dil, 2 * D_ATTN) for dil in DILATIONS]
                  + [kvt_spec, row(d_pool), pl.BlockSpec((1, POOL_HALO, d_pool), lambda bi, j: (bi, 0, 0)),
                     one(D_ATTN)],
        out_shape=[cls_shape(dil, D_ATTN) for dil in DILATIONS]
                  + [cls_shape(dil, 2 * D_ATTN) for dil in DILATIONS]
                  + [jax.ShapeDtypeStruct(kvt_all.shape, jnp.float32),
                     jax.ShapeDtypeStruct((n, d_pool), jnp.bfloat16),
                     jax.ShapeDtypeStruct((b, POOL_HALO, d_pool), jnp.float32),
                     jax.ShapeDtypeStruct((nb, 1, D_ATTN), jnp.float32)],
        scratch_shapes=[pltpu.VMEM((n_grp, tm, LANES), jnp.float32)] * 2
                       + [pltpu.VMEM((tm + POOL_HALO, d_pool), jnp.float32),
                          pltpu.VMEM((POOL_HALO, d_pool), jnp.float32)],
        input_output_aliases={6: 2 * n_pat},
        compiler_params=_params(2),
        name="inproj_prompt",
    )(x, g, w, sample_q.reshape(nb, 1, D_ATTN), sample_kv.reshape(nb, 1, 2 * D_ATTN), cache_t, kvt_all)
    return (outs[:n_pat], outs[n_pat:2 * n_pat], outs[2 * n_pat], outs[2 * n_pat + 1], outs[2 * n_pat + 2],
            outs[2 * n_pat + 3].reshape(nb, D_ATTN))


def _attn_body(dil, rblocks, out_groups, q_ref, kc_ref, kp_ref, o_ref, l_ref):
    assert out_groups == 1 or rblocks == 1
    n = pl.program_id(1)
    qi = lax.broadcasted_iota(jnp.int32, (QBLK, 2 * QBLK), 0)
    kj = lax.broadcasted_iota(jnp.int32, (QBLK, 2 * QBLK), 1) - QBLK
    rel = qi - kj
    band = (rel >= 0) & (rel <= SPAN)
    band_first = band & ((kj >= 0) | (n > 0))
    lane = lax.broadcasted_iota(jnp.int32, (QBLK, LANES), 1)
    low = lane < HEAD_DIM

    def one_block(r, j):
        mask = band_first if j == 0 else band
        q_rows = slice(j * QBLK, (j + 1) * QBLK)
        if out_groups == 1:
            start, step = j * QBLK * dil + r, dil
        else:
            start, step = (r % out_groups) * (QBLK * dil // out_groups) + r // out_groups, dil // out_groups
        rows = pl.ds(start, QBLK) if step == 1 else pl.ds(start, QBLK, stride=step)
        for p in range(D_ATTN // LANES):
            cols = slice(p * LANES, (p + 1) * LANES)
            vcols = slice(D_ATTN + p * LANES, D_ATTN + (p + 1) * LANES)
            qp = q_ref[0, r, q_rows, cols]
            if j == 0:
                k_prev, v_prev = kp_ref[0, r, :, cols], kp_ref[0, r, :, vcols]
            else:
                p_rows = slice((j - 1) * QBLK, j * QBLK)
                k_prev, v_prev = kc_ref[0, r, p_rows, cols], kc_ref[0, r, p_rows, vcols]
            kk = jnp.concatenate([k_prev, kc_ref[0, r, q_rows, cols]], axis=0)
            vv = jnp.concatenate([v_prev, kc_ref[0, r, q_rows, vcols]], axis=0)
            accs, ms, dens = [], [], []
            for sel in (low, ~low):
                qh = jnp.where(sel, qp, jnp.zeros_like(qp))
                s = _dot_nt(qh, kk)
                s = jnp.where(mask, s, NEG_INF)
                m = jnp.max(s, axis=-1, keepdims=True)
                e = jnp.exp2(s - m)
                ms.append(m)
                dens.append(jnp.sum(e, axis=-1, keepdims=True))
                accs.append(_dot(e.astype(jnp.bfloat16), vv))
            den = jnp.where(low, dens[0], dens[1])
            o_ref[0, p, rows, :] = jnp.where(low, accs[0], accs[1]) / den
            l_ref[0, p, rows, :] = jnp.where(low, ms[0], ms[1]) + jnp.log2(den)

    if dil <= ATTN_UNROLL:
        for r in range(dil):
            for j in range(rblocks):
                one_block(r, j)
    else:
        def classes(i, carry):
            for k in range(ATTN_UNROLL):
                for j in range(rblocks):
                    one_block(i * ATTN_UNROLL + k, j)
            return carry
        lax.fori_loop(0, dil // ATTN_UNROLL, classes, None)


def _attn_prompt(q, kv, dil, out_groups):
    b, _, sc, _ = q.shape
    rblocks = max(ATTN_UNROLL // dil, 1)
    rows = QBLK * rblocks
    nb = sc // rows
    cur = lambda c: pl.BlockSpec((1, dil, rows, c), lambda bi, n: (bi, 0, n, 0))
    prev = pl.BlockSpec((1, dil, QBLK, 2 * D_ATTN), lambda bi, n: (bi, 0, jnp.maximum(n * rblocks - 1, 0), 0))
    n_pair = D_ATTN // LANES
    out = pl.BlockSpec((1, n_pair, rows * dil, LANES), lambda bi, n: (bi, 0, n, 0))
    return pl.pallas_call(
        functools.partial(_attn_body, dil, rblocks, out_groups),
        grid=(b, nb),
        in_specs=[cur(D_ATTN), cur(2 * D_ATTN), prev],
        out_specs=[out, out],
        out_shape=[jax.ShapeDtypeStruct((b, n_pair, sc * dil, LANES), jnp.float32)] * 2,
        compiler_params=_params(2),
        name=f"attn_d{dil}",
    )(q, kv, kv)


def _to_column(row):
    return jnp.broadcast_to(row, (LANES, row.shape[1])).T[:, :1]


def _to_row(col):
    return jnp.broadcast_to(col, (col.shape[0], LANES)).T[:1, :]


def _sample_attn_step(q_ref, kvn_ref, c_ref, o_ref):
    w_buf = c_ref.shape[-1]
    pos = lax.broadcasted_iota(jnp.int32, (1, w_buf), 1)
    dist = w_buf - pos
    mult = jnp.zeros((1, w_buf), jnp.float32)
    for dil in DILATIONS:
        mult = mult + ((dist % dil == 0) & (dist <= SPAN * dil)).astype(jnp.float32)
    q_col = _to_column(q_ref[0])
    kvn_col = _to_column(kvn_ref[0])
    rows, news = [], []
    for h in range(N_HEADS):
        acc = None
        for g in range(HEAD_DIM // SUBLANES):
            ch = slice(h * HEAD_DIM + g * SUBLANES, h * HEAD_DIM + (g + 1) * SUBLANES)
            part = c_ref[0, 0, ch, :] * q_col[ch, :]
            acc = part if acc is None else acc + part
        rows.append(jnp.sum(acc, axis=0, keepdims=True))
        hs = slice(h * HEAD_DIM, (h + 1) * HEAD_DIM)
        news.append(jnp.sum(q_col[hs, :] * kvn_col[hs, :], axis=0, keepdims=True))
    s = jnp.where(mult > 0.0, jnp.concatenate(rows, axis=0), NEG_INF)
    s_new = jnp.concatenate(news, axis=0)
    m = jnp.maximum(jnp.max(s, axis=-1, keepdims=True), s_new)
    p = mult * jnp.exp(s - m)
    p_new = float(len(DILATIONS)) * jnp.exp(s_new - m)
    den = jnp.sum(p, axis=-1, keepdims=True) + p_new
    outs = []
    for h in range(N_HEADS):
        hs = slice(h * HEAD_DIM, (h + 1) * HEAD_DIM)
        vs = slice(D_ATTN + h * HEAD_DIM, D_ATTN + (h + 1) * HEAD_DIM)
        pv = jnp.sum(c_ref[0, 1, hs, :] * p[h:h + 1, :], axis=1, keepdims=True)
        outs.append((pv + p_new[h:h + 1, :] * kvn_col[vs, :]) / den[h:h + 1, :])
    o_ref[0] = _to_row(jnp.concatenate(outs, axis=0))


def _pool_project(x, attn, z, pw_ref, ps_ref, wo_ref):
    gdim = pw_ref.shape[1]
    zb = z.astype(jnp.bfloat16)
    pooled = jnp.concatenate(
        [_dot(zb[:, gi * gdim:(gi + 1) * gdim], pw_ref[gi]) for gi in range(pw_ref.shape[0])], axis=1)
    pooled = pooled * ps_ref[...]
    mix = jnp.concatenate([attn.astype(jnp.bfloat16), pooled.astype(jnp.bfloat16)], axis=1)
    return x + _dot(mix, wo_ref[...])


def _mlp_ple_math(x, p, gm_ref, wu_ref, wd_ref, gp_ref, wg_ref, wp_ref):
    xn = _rms(x, gm_ref[...]).astype(jnp.bfloat16)
    h = x
    for c in range(wu_ref.shape[1] // FF_CHUNK):
        a = jnp.maximum(_dot(xn, wu_ref[:, c * FF_CHUNK:(c + 1) * FF_CHUNK]), 0.0)
        h = h + _dot((a * a).astype(jnp.bfloat16), wd_ref[c * FF_CHUNK:(c + 1) * FF_CHUNK, :])
    gate = jax.nn.sigmoid(_dot(_rms(h, gp_ref[...]).astype(jnp.bfloat16), wg_ref[...]))
    return h + _dot(p.astype(jnp.bfloat16), wp_ref[...]) * gate


def _post_prompt_body(with_final, x_ref, o1_ref, l1_ref, o2_ref, l2_ref, o3_ref, l3_ref,
                      z_ref, p_ref, pw_ref, ps_ref, wo_ref, gm_ref, wu_ref, wd_ref, gp_ref,
                      wg_ref, wp_ref, gf_ref, h_ref, *rest):
    seqbuf = rest[-1]
    tm = x_ref.shape[0]
    groups = o3_ref.shape[3]

    def in_sequence(ref, slot, p):
        for g in range(groups):
            seqbuf[slot, pl.ds(g, tm // groups, stride=groups), :] = ref[0, p, 0, g]
        return seqbuf[slot]

    pairs = []
    for p in range(o1_ref.shape[1]):
        l1, l2, l3 = l1_ref[0, p], l2_ref[0, p], in_sequence(l3_ref, 2 * p, p)
        o3 = in_sequence(o3_ref, 2 * p + 1, p)
        m = jnp.maximum(jnp.maximum(l1, l2), l3)
        e1, e2, e3 = jnp.exp2(l1 - m), jnp.exp2(l2 - m), jnp.exp2(l3 - m)
        pairs.append((e1 * o1_ref[0, p] + e2 * o2_ref[0, p] + e3 * o3) / (e1 + e2 + e3))
    attn = jnp.concatenate(pairs, axis=1)

    h = _pool_project(x_ref[...], attn, z_ref[...], pw_ref, ps_ref, wo_ref)
    h = _mlp_ple_math(h, p_ref[...], gm_ref, wu_ref, wd_ref, gp_ref, wg_ref, wp_ref)
    h_ref[...] = h
    if with_final:
        rest[0][...] = _rms(h, gf_ref[...])


def _post_prompt(x, pats, z, p, layer, pool_w, pool_scale, w_out, g_mlp, w_up, w_down, g_ple, w_gate, w_ple,
                 g_final, seq, tm, with_final):
    n, d = x.shape
    d_pool = z.shape[1]
    row = lambda c: pl.BlockSpec((tm, c), lambda i: (i, 0))
    bps = seq // tm
    n_pair = D_ATTN // LANES
    pat = pl.BlockSpec((1, n_pair, tm, LANES), lambda i: (i // bps, 0, i % bps, 0))
    run = QBLK * DILATIONS[-1]
    per_run = run // tm
    last = [a.reshape(n // seq, n_pair, seq // run, LAST_GROUPS, run // LAST_GROUPS, LANES) for a in pats[-1]]
    pat_last = pl.BlockSpec((1, n_pair, 1, LAST_GROUPS, tm // LAST_GROUPS, LANES),
                            lambda i: (i // bps, 0, (i % bps) // per_run, 0, i % per_run, 0))
    flat = [a for pair in pats[:-1] for a in pair] + last
    p_spec = pl.BlockSpec((tm, p.shape[1]), lambda i: (layer * (n // tm) + i, 0))
    stacked = lambda a: _layer_spec(a, layer)
    params = (pool_w, pool_scale, w_out, g_mlp, w_up, w_down, g_ple, w_gate, w_ple)
    n_out = 2 if with_final else 1
    return pl.pallas_call(
        functools.partial(_post_prompt_body, with_final),
        grid=(n // tm,),
        in_specs=[row(d)] + [pat] * 4 + [pat_last] * 2 + [row(d_pool), p_spec]
                 + [stacked(a) for a in params] + [pl.BlockSpec((1, d), lambda i: (0, 0))],
        out_specs=[row(d)] * n_out,
        out_shape=[jax.ShapeDtypeStruct((n, d), jnp.float32)] * n_out,
        scratch_shapes=[pltpu.VMEM((2 * n_pair, tm, LANES), jnp.float32)],
        compiler_params=_params(1),
        name="post_prompt",
    )(x, *flat, z, p, *params, g_final)


def _sample_tail_body(with_final, x_ref, attn_ref, u_ref, st_ref, p_ref, pw_ref, ps_ref, wo_ref, gm_ref,
                      wu_ref, wd_ref, gp_ref, wg_ref, wp_ref, gf_ref, h_ref, *rest):
    pool_ref, xn_buf, acc = rest[-3:]
    c = pl.program_id(0)

    @pl.when(c == 0)
    def _():
        u = u_ref[...]
        n_st = st_ref.shape[0]
        gdim = u.shape[1] // len(POOL_WINDOWS)
        zs = []
        for gi, w in enumerate(POOL_WINDOWS):
            cols = slice(gi * gdim, (gi + 1) * gdim)
            cur = u[:, cols]
            tot = cur
            for k in range(1, w):
                tot = tot + st_ref[n_st - k, :, cols]
            zs.append(tot / float(w) - cur)
        h1 = _pool_project(x_ref[...], attn_ref[...], jnp.concatenate(zs, axis=1), pw_ref, ps_ref, wo_ref)
        acc[...] = h1
        xn_buf[...] = _rms(h1, gm_ref[...]).astype(jnp.bfloat16)
        pool_ref[:n_st - 1] = st_ref[1:]
        pool_ref[n_st - 1] = u

    a = jnp.maximum(_dot(xn_buf[...], wu_ref[...]), 0.0)
    acc[...] += _dot((a * a).astype(jnp.bfloat16), wd_ref[...])

    @pl.when(c == pl.num_programs(0) - 1)
    def _():
        h = acc[...]
        gate = jax.nn.sigmoid(_dot(_rms(h, gp_ref[...]).astype(jnp.bfloat16), wg_ref[...]))
        h = h + _dot(p_ref[...].astype(jnp.bfloat16), wp_ref[...]) * gate
        h_ref[...] = h
        if with_final:
            rest[0][...] = _rms(h, gf_ref[...])


def _sample_tail(x, attn, u, state_t, p, layer, pool_w, pool_scale, w_out, g_mlp, w_up, w_down, g_ple, w_gate,
                 w_ple, g_final, with_final):
    n, d = x.shape
    _, n_st, _, d_pool = state_t.shape
    assert n_st == POOL_WINDOWS[-1] - 1
    full = lambda a: pl.BlockSpec(a.shape, lambda c: (0,) * a.ndim)
    stacked = lambda a: _layer_spec(a, layer)
    up_spec = pl.BlockSpec((None, d, FF_CHUNK), lambda c: (layer, 0, c))
    down_spec = pl.BlockSpec((None, FF_CHUNK, d), lambda c: (layer, c, 0))
    p_spec = pl.BlockSpec((n, p.shape[1]), lambda c: (layer, 0))
    n_out = 2 if with_final else 1
    outs = pl.pallas_call(
        functools.partial(_sample_tail_body, with_final),
        grid=(w_up.shape[-1] // FF_CHUNK,),
        in_specs=[full(x), full(attn), full(u), stacked(state_t), p_spec, stacked(pool_w), stacked(pool_scale),
                  stacked(w_out), stacked(g_mlp), up_spec, down_spec, stacked(g_ple), stacked(w_gate),
                  stacked(w_ple), full(g_final)],
        out_specs=[pl.BlockSpec((n, d), lambda c: (0, 0))] * n_out
                  + [pl.BlockSpec((n_st, n, d_pool), lambda c: (0, 0, 0))],
        out_shape=[jax.ShapeDtypeStruct((n, d), jnp.float32)] * n_out
                  + [jax.ShapeDtypeStruct((n_st, n, d_pool), jnp.float32)],
        scratch_shapes=[pltpu.VMEM((n, d), jnp.bfloat16), pltpu.VMEM((n, d), jnp.float32)],
        compiler_params=_params(1),
        name="sample_tail",
    )(x, attn, u, state_t, p, pool_w, pool_scale, w_out, g_mlp, w_up, w_down, g_ple, w_gate, w_ple, g_final)
    return outs[:n_out], outs[n_out]


def kernel(x_prompt, x_sample, cache_attn_kv, state_pool, p_prompt, p_sample, norm_attn_g, w_in, pool_w,
           pool_scale, w_out, norm_mlp_g, w_up, w_down, ple_norm_g, w_ple_gate, w_ple, final_norm_g):
    b, s, d = x_prompt.shape
    nb_s, t_s, _ = x_sample.shape
    depth = w_in.shape[0]
    d_pool = state_pool.shape[-1]
    n_st = state_pool.shape[2]
    w_buf = cache_attn_kv.shape[2]
    kv_keep = min(SPAN * DILATIONS[-1], s)
    assert t_s == 1 and s % (QBLK * DILATIONS[-1]) == 0
    tm = 512
    bf = lambda a: a.astype(jnp.bfloat16)
    rows = lambda a: a.reshape(depth, 1, -1)

    wi, wo, wu, wd = bf(w_in), bf(w_out), bf(w_up), bf(w_down)
    wg, wp, pw = bf(w_ple_gate), bf(w_ple), bf(pool_w)
    ga, gm, gp, ps = rows(norm_attn_g), rows(norm_mlp_g), rows(ple_norm_g), rows(pool_scale)
    g_final = final_norm_g.reshape(1, -1)

    hp = x_prompt.reshape(b * s, d)
    hs = x_sample.reshape(nb_s, d)
    pp_all = p_prompt.reshape(depth * b * s, -1)
    ps_all = p_sample.reshape(depth * nb_s, -1)
    cache_t = cache_attn_kv.transpose(0, 1, 3, 4, 5, 2).reshape(depth * nb_s, 2, D_ATTN, w_buf)
    state_t = state_pool.transpose(0, 2, 1, 3)
    kvt_all = jnp.zeros((depth * b, 2 * D_ATTN, kv_keep), jnp.float32)
    kv_s, pool_p, pool_s = [], [], []
    yp = ys = None
    for i in range(depth):
        last = i == depth - 1

        sq, kvf, su = _inproj_sample(hs, ga, wi, i)
        qs, kvs, kvt_all, z, u_tail, sattn = _inproj_prompt(hp, ga, wi, i, kvt_all, sq, kvf, cache_t, tm, s,
                                                            kv_keep)

        groups = [1] * (len(DILATIONS) - 1) + [LAST_GROUPS]
        pats = [_attn_prompt(q, kv, dil, og) for q, kv, dil, og in zip(qs, kvs, DILATIONS, groups)]
        outs = _post_prompt(hp, pats, z, pp_all, i, pw, ps, wo, gm, wu, wd, gp, wg, wp, g_final, s, tm, last)
        hp = outs[0]
        if last:
            yp = outs[1]
        pool_p.append(u_tail[:, POOL_HALO - n_st:])

        outs, pool_new = _sample_tail(hs, sattn, su, state_t, ps_all, i, pw, ps, wo, gm, wu, wd, gp, wg, wp,
                                      g_final, last)
        hs = outs[0]
        if last:
            ys = outs[1]
        kv_s.append(kvf.reshape(nb_s, 1, 2, N_HEADS, HEAD_DIM))
        pool_s.append(pool_new)

    kv_prompt = kvt_all.reshape(depth, b, 2, N_HEADS, HEAD_DIM, kv_keep).transpose(0, 1, 5, 2, 3, 4)
    pool_sample = jnp.stack(pool_s).transpose(0, 2, 1, 3)
    return (yp.reshape(b, s, d), ys.reshape(nb_s, 1, d), kv_prompt, jnp.stack(kv_s),
            jnp.stack(pool_p), pool_sample)
```
